```python
import math
import jax, jax.numpy as jnp
from jax import lax
import numpy as np

D_MODEL = 1024
BATCH = 8
SEQ = 2048
DEPTH = 1
DEC_BATCH = 32
DEC_SEQ = 8
PAST_LEN = 16384
PAGE_SIZE = 128

HEAD_DIM = 64
N_HEADS_A = 16
W_A = N_HEADS_A * HEAD_DIM
W_R = D_MODEL
N_LRU_BLOCKS = 16
LRU_BLOCK = W_R // N_LRU_BLOCKS
D_MIX = W_A + W_R
D_PROJ = 4 * W_A + 2 * W_R
CONV_W = 4
LRU_C = 8.0
DILATIONS = (1, 4, 16)
N_STEPS = 128
WINDOW_MAX = 2048
BLOCK = 128
N_BUCKETS = 32
MAX_EXACT = 16
MAX_DISTANCE = 2048
EPS = 1e-6
SCALE = HEAD_DIM ** -0.5
NEG_INF = -1e30

kernel_name = "hymba_dilated_swa_rglru_step"


def rmsnorm(x, g):
    xf = x.astype(jnp.float32)
    y = xf * lax.rsqrt(jnp.mean(xf * xf, axis=-1, keepdims=True) + EPS)
    return (y * g.astype(jnp.float32)).astype(x.dtype)


def t5_bucket(dist):
    nf = jnp.maximum(dist, 1).astype(jnp.float32)
    large = MAX_EXACT + (jnp.log(nf / MAX_EXACT) / math.log(MAX_DISTANCE / MAX_EXACT)
                         * (N_BUCKETS - MAX_EXACT)).astype(jnp.int32)
    large = jnp.minimum(large, N_BUCKETS - 1)
    return jnp.where(dist < MAX_EXACT, dist, large)


def dilated_band_prompt(q, k, v, rel_bias, dil):
    B, L, H, Dh = q.shape
    span = dil * BLOCK
    Lp = -(-L // span) * span
    nb = Lp // span

    def to_sub(x):
        x = jnp.pad(x, ((0, 0), (0, Lp - L), (0, 0), (0, 0)))
        x = x.reshape(B, Lp // dil, dil, H, Dh).transpose(0, 2, 1, 3, 4)
        return x.reshape(B, dil, nb, BLOCK, H, Dh)

    qs, ks, vs = to_sub(q), to_sub(k), to_sub(v)

    def with_prev(x):
        prev = jnp.pad(x, ((0, 0), (0, 0), (1, 0), (0, 0), (0, 0), (0, 0)))[:, :, :nb]
        return jnp.concatenate([prev, x], axis=3)

    kb, vb = with_prev(ks), with_prev(vs)
    i = jnp.arange(BLOCK)[:, None]
    j = jnp.arange(2 * BLOCK)[None, :]
    diff = i - j + BLOCK
    band = (diff >= 0) & (diff <= N_STEPS)
    first = (jnp.arange(nb) > 0)[:, None, None] | (j >= BLOCK)[None]
    valid = band[None] & first
    bias = rel_bias[t5_bucket(jnp.maximum(diff, 0) * dil)].transpose(2, 0, 1)
    logits = jnp.einsum('brnqhd,brnkhd->brnhqk', qs, kb).astype(jnp.float32) * SCALE
    logits = logits + bias.astype(jnp.float32)
    logits = jnp.where(valid[None, None, :, None], logits, NEG_INF)
    lse = jax.nn.logsumexp(logits, axis=-1)
    p = jnp.exp(logits - lse[..., None]).astype(v.dtype)
    o = jnp.einsum('brnhqk,brnkhd->brnqhd', p, vb)
    o = o.reshape(B, dil, Lp // dil, H, Dh).transpose(0, 2, 1, 3, 4).reshape(B, Lp, H, Dh)[:, :L]
    lse = lse.transpose(0, 1, 2, 4, 3).reshape(B, dil, Lp // dil, H)
    lse = lse.transpose(0, 2, 1, 3).reshape(B, Lp, H)[:, :L]
    return o, lse


def dilated_gather_sample(q, k_ctx, v_ctx, rel_bias, dil, q_start):
    T = q.shape[1]
    steps = jnp.arange(N_STEPS + 1)
    idx = q_start + jnp.arange(T)[:, None] - dil * steps[None, :]
    valid = idx >= 0
    idx_c = jnp.maximum(idx, 0)
    kg = k_ctx[:, idx_c]
    vg = v_ctx[:, idx_c]
    bias = rel_bias[t5_bucket(dil * steps)].T[:, None, :]
    logits = jnp.einsum('bthd,btshd->bhts', q, kg).astype(jnp.float32) * SCALE
    logits = logits + bias.astype(jnp.float32)
    logits = jnp.where(valid[None, None], logits, NEG_INF)
    lse = jax.nn.logsumexp(logits, axis=-1)
    p = jnp.exp(logits - lse[..., None]).astype(v_ctx.dtype)
    o = jnp.einsum('bhts,btshd->bthd', p, vg)
    return o, lse.transpose(0, 2, 1)


def mixer_layer(x, k_past, v_past, conv_past, h_past, rel_bias, norm_in, w_in, norm_attn,
                norm_lru, conv_w, conv_b, w_gate_x, b_gate_x, w_gate_a, b_gate_a, lru_param, w_out):
    B, T, _ = x.shape
    h = rmsnorm(x, norm_in)
    proj = h @ w_in
    q, k, v, g_a, x_r, g_r = jnp.split(
        proj, [W_A, 2 * W_A, 3 * W_A, 4 * W_A, 4 * W_A + W_R], axis=-1)
    q = q.reshape(B, T, N_HEADS_A, HEAD_DIM)
    k = k.reshape(B, T, N_HEADS_A, HEAD_DIM)
    v = v.reshape(B, T, N_HEADS_A, HEAD_DIM)

    if k_past is None:
        outs = [dilated_band_prompt(q, k, v, rel_bias, d) for d in DILATIONS]
        nkeep = min(WINDOW_MAX, T)
        new_k, new_v = k[:, -nkeep:], v[:, -nkeep:]
    else:
        wb = k_past.shape[1]
        k_ctx = jnp.concatenate([k_past, k], axis=1)
        v_ctx = jnp.concatenate([v_past, v], axis=1)
        outs = [dilated_gather_sample(q, k_ctx, v_ctx, rel_bias, d, wb) for d in DILATIONS]
        new_k, new_v = k_ctx[:, -wb:], v_ctx[:, -wb:]
    lses = jnp.stack([o_l[1] for o_l in outs], axis=0)
    wts = jax.nn.softmax(lses, axis=0)
    o_att = jnp.sum(wts[..., None] * jnp.stack([o_l[0] for o_l in outs], axis=0).astype(jnp.float32), axis=0)
    o_att = o_att.astype(x.dtype).reshape(B, T, W_A)
    y_att = rmsnorm(o_att, norm_attn) * jax.nn.silu(g_a)

    xpad = jnp.concatenate([conv_past.astype(x_r.dtype), x_r], axis=1)
    xc = conv_b + sum(conv_w[tap] * xpad[:, tap:tap + T] for tap in range(CONV_W))
    new_conv = xpad[:, -(CONV_W - 1):]
    xb = xc.reshape(B, T, N_LRU_BLOCKS, LRU_BLOCK)
    gate_x = jax.nn.sigmoid(jnp.einsum('bthi,hij->bthj', xb, w_gate_x).reshape(B, T, W_R) + b_gate_x)
    gate_a = jax.nn.sigmoid(jnp.einsum('bthi,hij->bthj', xb, w_gate_a).reshape(B, T, W_R) + b_gate_a)
    log_a = -LRU_C * gate_a.astype(jnp.float32) * jax.nn.softplus(-lru_param.astype(jnp.float32))
    a = jnp.exp(log_a)
    bx = jnp.sqrt(-jnp.expm1(2.0 * log_a)) * (gate_x * xc).astype(jnp.float32)

    def step(hc, ab):
        a_t, b_t = ab
        hc = a_t * hc + b_t
        return hc, hc

    h_last, hs = lax.scan(step, h_past.astype(jnp.float32),
                          (a.transpose(1, 0, 2), bx.transpose(1, 0, 2)))
    o_lru = hs.transpose(1, 0, 2).astype(x.dtype)
    y_lru = rmsnorm(o_lru, norm_lru) * jax.nn.silu(g_r)

    y = x + jnp.concatenate([y_att, y_lru], axis=-1) @ w_out
    return y, new_k, new_v, new_conv, h_last.astype(h_past.dtype)


def setup_inputs(seed: int = 0) -> dict:
    key = jax.random.key(seed)
    ks = jax.random.split(key, 20)
    wb = min(WINDOW_MAX, PAST_LEN)
    nrm = jax.random.normal
    a0 = jax.random.uniform(ks[16], (DEPTH, W_R), minval=0.9, maxval=0.999)
    s = a0 ** (1.0 / LRU_C)
    return {
        'x_prompt': nrm(ks[0], (BATCH, SEQ, D_MODEL), jnp.float32),
        'x_sample': nrm(ks[1], (DEC_BATCH, DEC_SEQ, D_MODEL), jnp.float32),
        'cache_win_k': nrm(ks[2], (DEPTH, DEC_BATCH, wb, N_HEADS_A, HEAD_DIM), jnp.float32),
        'cache_win_v': nrm(ks[3], (DEPTH, DEC_BATCH, wb, N_HEADS_A, HEAD_DIM), jnp.float32),
        'state_conv': nrm(ks[4], (DEPTH, DEC_BATCH, CONV_W - 1, W_R), jnp.float32),
        'state_lru': 0.5 * nrm(ks[5], (DEPTH, DEC_BATCH, W_R), jnp.float32),
        'rel_bias': 0.1 * nrm(ks[6], (N_BUCKETS, N_HEADS_A), jnp.float32),
        'norm_in': 1.0 + 0.01 * nrm(ks[7], (DEPTH, D_MODEL), jnp.float32),
        'w_in': nrm(ks[8], (DEPTH, D_MODEL, D_PROJ), jnp.float32) * D_MODEL ** -0.5,
        'norm_attn': 1.0 + 0.01 * nrm(ks[9], (DEPTH, W_A), jnp.float32),
        'norm_lru': 1.0 + 0.01 * nrm(ks[10], (DEPTH, W_R), jnp.float32),
        'conv_w': nrm(ks[11], (DEPTH, CONV_W, W_R), jnp.float32) * CONV_W ** -0.5,
        'conv_b': 0.01 * nrm(ks[12], (DEPTH, W_R), jnp.float32),
        'w_gate_x': nrm(ks[13], (DEPTH, N_LRU_BLOCKS, LRU_BLOCK, LRU_BLOCK), jnp.float32) * LRU_BLOCK ** -0.5,
        'b_gate_x': 0.01 * nrm(ks[14], (DEPTH, W_R), jnp.float32),
        'w_gate_a': nrm(ks[15], (DEPTH, N_LRU_BLOCKS, LRU_BLOCK, LRU_BLOCK), jnp.float32) * LRU_BLOCK ** -0.5,
        'b_gate_a': 0.01 * nrm(ks[17], (DEPTH, W_R), jnp.float32),
        'lru_param': jnp.log(s) - jnp.log1p(-s),
        'w_out': nrm(ks[18], (DEPTH, D_MIX, D_MODEL), jnp.float32) * D_MIX ** -0.5,
        'norm_final': 1.0 + 0.01 * nrm(ks[19], (D_MODEL,), jnp.float32),
    }


def reference(x_prompt, x_sample, cache_win_k, cache_win_v, state_conv, state_lru, rel_bias,
              norm_in, w_in, norm_attn, norm_lru, conv_w, conv_b, w_gate_x, b_gate_x,
              w_gate_a, b_gate_a, lru_param, w_out, norm_final):
    xp, xs = x_prompt, x_sample
    kp_l, vp_l, cp_l, hp_l = [], [], [], []
    ks_l, vs_l, cs_l, hs_l = [], [], [], []
    for l in range(DEPTH):
        layer_w = (rel_bias, norm_in[l], w_in[l], norm_attn[l], norm_lru[l], conv_w[l], conv_b[l],
                   w_gate_x[l], b_gate_x[l], w_gate_a[l], b_gate_a[l], lru_param[l], w_out[l])
        conv0 = jnp.zeros((xp.shape[0], CONV_W - 1, W_R), xp.dtype)
        h0 = jnp.zeros((xp.shape[0], W_R), state_lru.dtype)
        xp, kp, vp, cp, hp = mixer_layer(xp, None, None, conv0, h0, *layer_w)
        xs, ks_, vs_, cs_, hs_ = mixer_layer(xs, cache_win_k[l], cache_win_v[l], state_conv[l],
                                             state_lru[l], *layer_w)
        kp_l.append(kp); vp_l.append(vp); cp_l.append(cp); hp_l.append(hp)
        ks_l.append(ks_); vs_l.append(vs_); cs_l.append(cs_); hs_l.append(hs_)
    y_prompt = rmsnorm(xp, norm_final)
    y_sample = rmsnorm(xs, norm_final)
    return (y_prompt, y_sample,
            jnp.stack(kp_l), jnp.stack(vp_l), jnp.stack(cp_l), jnp.stack(hp_l),
            jnp.stack(ks_l), jnp.stack(vs_l), jnp.stack(cs_l), jnp.stack(hs_l))
```

```python
import functools
import math

import numpy as np
import jax
import jax.numpy as jnp
from jax import lax
from jax.experimental import pallas as pl
from jax.experimental.pallas import tpu as pltpu

D_MODEL = 1024
HEAD_DIM = 64
N_HEADS = 16
W_A = N_HEADS * HEAD_DIM
W_R = D_MODEL
N_LRU_BLOCKS = 16
LRU_BLOCK = W_R // N_LRU_BLOCKS
N_PROJ = 6
CONV_W = 4
LRU_C = 8.0
DILATIONS = (1, 4, 16)
N_STEPS = 128
BLOCK = 128
N_BUCKETS = 32
MAX_EXACT = 16
MAX_DISTANCE = 2048
EPS = 1e-6
SCALE = HEAD_DIM ** -0.5
NEG_INF = -1e30

LANES = 128
SUBLANES = 8
HEADS_PER_LANE_TILE = LANES // HEAD_DIM
MXU_TILE = 256
VMEM_LIMIT = 56 * 1024 * 1024

F32 = jnp.float32
BF16 = jnp.bfloat16


def _params(n_axes):
    return pltpu.CompilerParams(dimension_semantics=("arbitrary",) * n_axes,
                                vmem_limit_bytes=VMEM_LIMIT)


def _rms_scale(x):
    return lax.rsqrt(jnp.mean(x * x, axis=-1, keepdims=True) + EPS)


def _inproj_body(x_ref, g_ref, w_ref, *out_refs):
    x = x_ref[...]
    h = ((x * _rms_scale(x)) * g_ref[...]).astype(BF16)
    for i, o_ref in enumerate(out_refs):
        r = jnp.dot(h, w_ref[:, i * W_A:(i + 1) * W_A], preferred_element_type=F32)
        if i == 0:
            r = r * SCALE
        o_ref[...] = r


def _inproj(x2d, norm_in, w_in_bf16, tm):
    m = x2d.shape[0]
    row = pl.BlockSpec((tm, D_MODEL), lambda i: (i, 0))
    return pl.pallas_call(
        _inproj_body,
        grid=(m // tm,),
        in_specs=[row,
                  pl.BlockSpec((1, D_MODEL), lambda i: (0, 0)),
                  pl.BlockSpec((D_MODEL, N_PROJ * W_A), lambda i: (0, 0))],
        out_specs=[row] * N_PROJ,
        out_shape=[jax.ShapeDtypeStruct((m, W_A), F32)] * N_PROJ,
        compiler_params=_params(1),
        name="inproj",
    )(x2d, norm_in, w_in_bf16)


def _t5_bucket_np(dist):
    nf = np.maximum(dist, 1).astype(np.float32)
    large = MAX_EXACT + (np.log(nf / np.float32(MAX_EXACT)) / np.float32(math.log(MAX_DISTANCE / MAX_EXACT))
                         * np.float32(N_BUCKETS - MAX_EXACT)).astype(np.int32)
    large = np.minimum(large, N_BUCKETS - 1)
    return np.where(dist < MAX_EXACT, dist, large).astype(np.int32)


def _prompt_bias_tables(rel_bias):
    i = np.arange(BLOCK)[:, None]
    j = np.arange(2 * BLOCK)[None, :]
    diff = i - j + BLOCK
    band = (diff >= 0) & (diff <= N_STEPS)
    tabs = []
    for dil in DILATIONS:
        idx = _t5_bucket_np(np.maximum(diff, 0) * dil)
        t = jnp.transpose(rel_bias[idx], (2, 0, 1)).astype(F32)
        tabs.append(jnp.where(band[None], t, NEG_INF))
    return jnp.stack(tabs)


def _sample_bias_tables(rel_bias, n_new, wb, n_ctx_pad):
    t = np.arange(n_new)[:, None]
    j = np.arange(n_ctx_pad)[None, :]
    dist = wb + t - j
    tabs = []
    for dil in DILATIONS:
        valid = (dist >= 0) & (dist % dil == 0) & (dist // dil <= N_STEPS) & (j < wb + n_new)
        idx = _t5_bucket_np(np.maximum(dist, 0))
        tb = jnp.transpose(rel_bias[idx], (2, 0, 1)).astype(F32)
        tb = jnp.where(valid[None], tb, NEG_INF)
        tabs.append(tb.reshape(N_HEADS * n_new, n_ctx_pad))
    return jnp.stack(tabs)


def _lane_is_first_head():
    return lax.broadcasted_iota(jnp.int32, (BLOCK, LANES), 1) < HEAD_DIM


def _attn_block(qb, kb, vb, bias_a, bias_b):
    first = _lane_is_first_head()
    outs = []
    for is_first, bias in ((True, bias_a), (False, bias_b)):
        qh = (jnp.where(first, qb, 0.0) if is_first else jnp.where(first, 0.0, qb)).astype(BF16)
        s = lax.dot_general(qh, kb, (((1,), (1,)), ((), ())), preferred_element_type=F32) + bias
        m = jnp.max(s, axis=-1, keepdims=True)
        e = jnp.exp(s - m)
        l = jnp.sum(e, axis=-1, keepdims=True)
        acc = jnp.dot(e.astype(BF16), vb, preferred_element_type=F32)
        outs.append((acc, m, l))
    (acc_a, m_a, l_a), (acc_b, m_b, l_b) = outs
    return (jnp.where(first, acc_a, acc_b), jnp.where(first, m_a, m_b), jnp.where(first, l_a, l_b))


def _attn_prompt_body(q_ref, k_ref, v_ref, bm_ref, o_ref,
                      acc4_ref, m4_ref, l4_ref, acc16_ref, m16_ref, l16_ref, *, seq):
    def load_f32(ref, start, n, stride):
        if stride == 1:
            return ref[pl.ds(start, n), :]
        return ref[pl.ds(start, n, stride=stride), :]

    def load(ref, start, n, stride):
        return load_f32(ref, start, n, stride).astype(BF16)

    def block(dil_idx, dil, q_start, has_prev):
        qb = load_f32(q_ref, q_start, BLOCK, dil)
        if has_prev:
            k_start = q_start - dil * BLOCK
            kb = load(k_ref, k_start, 2 * BLOCK, dil)
            vb = load(v_ref, k_start, 2 * BLOCK, dil)
            ba, bb = bm_ref[dil_idx, 0], bm_ref[dil_idx, 1]
        else:
            kb = load(k_ref, q_start, BLOCK, dil)
            vb = load(v_ref, q_start, BLOCK, dil)
            ba, bb = bm_ref[dil_idx, 0, :, BLOCK:], bm_ref[dil_idx, 1, :, BLOCK:]
        return _attn_block(qb, kb, vb, ba, bb)

    def store(refs, start, dil, vals):
        for ref, val in zip(refs, vals):
            ref[pl.ds(start, BLOCK, stride=dil), :] = val

    dil = DILATIONS[2]
    n_sub_blocks = seq // (dil * BLOCK)
    assert n_sub_blocks == 1

    def d16_body(r, carry):
        store((acc16_ref, m16_ref, l16_ref), r, dil, block(2, dil, r, False))
        return carry
    lax.fori_loop(0, dil, d16_body, 0)

    dil4 = DILATIONS[1]
    nb4 = seq // (dil4 * BLOCK)

    def d4_body(r, carry):
        store((acc4_ref, m4_ref, l4_ref), r, dil4, block(1, dil4, r, False))

        def inner(n, c):
            start = r + n * (dil4 * BLOCK)
            store((acc4_ref, m4_ref, l4_ref), start, dil4, block(1, dil4, start, True))
            return c
        lax.fori_loop(1, nb4, inner, 0)
        return carry
    lax.fori_loop(0, dil4, d4_body, 0)

    def finish(start, vals):
        acc1, m1, l1 = vals
        rows = pl.ds(start, BLOCK)
        m4, m16 = m4_ref[rows, :], m16_ref[rows, :]
        mx = jnp.maximum(jnp.maximum(m1, m4), m16)
        a1, a4, a16 = jnp.exp(m1 - mx), jnp.exp(m4 - mx), jnp.exp(m16 - mx)
        num = a1 * acc1 + a4 * acc4_ref[rows, :] + a16 * acc16_ref[rows, :]
        den = a1 * l1 + a4 * l4_ref[rows, :] + a16 * l16_ref[rows, :]
        o_ref[rows, :] = num / den

    finish(0, block(0, 1, 0, False))

    def d1_body(n, carry):
        start = pl.multiple_of(n * BLOCK, BLOCK)
        finish(start, block(0, 1, start, True))
        return carry
    lax.fori_loop(1, seq // BLOCK, d1_body, 0)


def _attn_prompt(q, k, v, bias_tabs):
    b, seq, _ = q.shape
    n_pairs = W_A // LANES
    tile = pl.BlockSpec((None, seq, LANES), lambda p, i: (i, 0, p))
    scratch = [pltpu.VMEM((seq, LANES), F32)] * 6
    return pl.pallas_call(
        functools.partial(_attn_prompt_body, seq=seq),
        grid=(n_pairs, b),
        in_specs=[tile, tile, tile,
                  pl.BlockSpec((len(DILATIONS), HEADS_PER_LANE_TILE, BLOCK, 2 * BLOCK), lambda p, i: (0, p, 0, 0))],
        out_specs=tile,
        out_shape=jax.ShapeDtypeStruct((b, seq, W_A), F32),
        scratch_shapes=scratch,
        compiler_params=_params(2),
        name="attn_prompt",
    )(q, k, v, bias_tabs)


SAMPLE_HEADS = 8
SAMPLE_LANES = SAMPLE_HEADS * HEAD_DIM
COPY_ROWS = 256


def _attn_sample_body(q_ref, kn_ref, vn_ref, ck_ref, cv_ref, bm_ref, o_ref, nk_ref, nv_ref,
                      kctx_ref, vctx_ref, *, wb, n_new, n_ctx_pad):
    pad = jnp.zeros((n_ctx_pad - wb - n_new, SAMPLE_LANES), F32)
    for c_ref, new_ref, out_ref, ctx_ref in ((ck_ref, kn_ref, nk_ref, kctx_ref), (cv_ref, vn_ref, nv_ref, vctx_ref)):
        for c in range(wb // COPY_ROWS):
            blk = c_ref[c * COPY_ROWS:(c + 1) * COPY_ROWS, :]
            ctx_ref[c * COPY_ROWS:(c + 1) * COPY_ROWS, :] = blk.astype(BF16)
            if c == 0:
                out_ref[0:COPY_ROWS - n_new, :] = blk[n_new:, :]
            else:
                out_ref[c * COPY_ROWS - n_new:(c + 1) * COPY_ROWS - n_new, :] = blk
        new = new_ref[...]
        out_ref[wb - n_new:wb, :] = new
        ctx_ref[wb:n_ctx_pad, :] = jnp.concatenate([new, pad], axis=0).astype(BF16)

    rows = SAMPLE_HEADS * n_new
    row_head = lax.broadcasted_iota(jnp.int32, (rows, SAMPLE_LANES), 0) // n_new
    col_head = lax.broadcasted_iota(jnp.int32, (rows, SAMPLE_LANES), 1) // HEAD_DIM
    q_rep = jnp.concatenate([q_ref[...]] * SAMPLE_HEADS, axis=0)
    q_bd = jnp.where(row_head == col_head, q_rep, 0.0).astype(BF16)
    s = lax.dot_general(q_bd, kctx_ref[...], (((1,), (1,)), ((), ())), preferred_element_type=F32)
    logits = [s + bm_ref[d] for d in range(len(DILATIONS))]
    mx = functools.reduce(jnp.maximum, [jnp.max(t, axis=-1, keepdims=True) for t in logits])
    e = functools.reduce(lambda a, b: a + b, [jnp.exp(t - mx) for t in logits])
    den = jnp.sum(e, axis=-1, keepdims=True)
    o_full = jnp.dot(e.astype(BF16), vctx_ref[...], preferred_element_type=F32) / den
    o_full = jnp.where(row_head == col_head, o_full, 0.0)
    out = o_full[0:n_new, :]
    for h in range(1, SAMPLE_HEADS):
        out = out + o_full[h * n_new:(h + 1) * n_new, :]
    o_ref[...] = out


def _attn_sample(q, k_new, v_new, cache_k, cache_v, bias_tabs):
    b, n_new, _ = q.shape
    wb = cache_k.shape[1]
    n_ctx_pad = bias_tabs.shape[-1]
    n_groups = N_HEADS // SAMPLE_HEADS
    new_spec = pl.BlockSpec((None, n_new, SAMPLE_LANES), lambda i, g: (i, 0, g))
    cache_spec = pl.BlockSpec((None, wb, SAMPLE_LANES), lambda i, g: (i, 0, g))
    return pl.pallas_call(
        functools.partial(_attn_sample_body, wb=wb, n_new=n_new, n_ctx_pad=n_ctx_pad),
        grid=(b, n_groups),
        in_specs=[new_spec, new_spec, new_spec, cache_spec, cache_spec,
                  pl.BlockSpec((len(DILATIONS), SAMPLE_HEADS * n_new, n_ctx_pad), lambda i, g: (0, g, 0))],
        out_specs=[new_spec, cache_spec, cache_spec],
        out_shape=[jax.ShapeDtypeStruct((b, n_new, W_A), F32),
                   jax.ShapeDtypeStruct((b, wb, W_A), F32),
                   jax.ShapeDtypeStruct((b, wb, W_A), F32)],
        scratch_shapes=[pltpu.VMEM((n_ctx_pad, SAMPLE_LANES), BF16)] * 2,
        compiler_params=_params(2),
        name="attn_sample",
    )(q, k_new, v_new, cache_k, cache_v, bias_tabs)


def _lru_body(xr_ref, gr_ref, cpast_ref, hpast_ref, cw_ref, cb_ref, wg_ref, bgx_ref, bga_ref, lp_ref, nl_ref,
              y_ref, nconv_ref, nlru_ref, tail_ref, h_ref, hs_ref, *, tt):
    t = pl.program_id(1)
    n_t = pl.num_programs(1)

    @pl.when(t == 0)
    def _():
        tail_ref[...] = jnp.concatenate(
            [jnp.zeros((SUBLANES - (CONV_W - 1), W_R), F32), cpast_ref[...]], axis=0)
        h_ref[...] = hpast_ref[...]

    x = xr_ref[...]
    xe = jnp.concatenate([tail_ref[...], x], axis=0)
    xc = cb_ref[...] + cw_ref[CONV_W - 1:CONV_W, :] * x
    for tap in range(CONV_W - 1):
        back = CONV_W - 1 - tap
        xc = xc + cw_ref[tap:tap + 1, :] * xe[SUBLANES - back:SUBLANES - back + tt, :]
    tail_ref[...] = x[tt - SUBLANES:tt, :]

    xcb = xc.astype(BF16)
    gx, ga = [], []
    for j in range(W_R // MXU_TILE):
        g = jnp.dot(xcb[:, j * MXU_TILE:(j + 1) * MXU_TILE], wg_ref[j], preferred_element_type=F32)
        gx.append(g[:, :MXU_TILE])
        ga.append(g[:, MXU_TILE:])
    gate_x = jax.nn.sigmoid(jnp.concatenate(gx, axis=-1) + bgx_ref[...])
    gate_a = jax.nn.sigmoid(jnp.concatenate(ga, axis=-1) + bga_ref[...])
    log_a = (-LRU_C) * gate_a * jax.nn.softplus(-lp_ref[...])
    a = jnp.exp(log_a)
    bx = jnp.sqrt(jnp.tanh(-log_a) * (1.0 + a * a)) * (gate_x * xc)

    groups = tt // SUBLANES
    a3 = a.reshape(groups, SUBLANES, W_R)
    b3 = bx.reshape(groups, SUBLANES, W_R)
    sub = lax.broadcasted_iota(jnp.int32, (groups, SUBLANES, W_R), 1)
    shift = 1
    while shift < SUBLANES:
        keep = sub >= shift
        b_prev = jnp.where(keep, pltpu.roll(b3, shift, axis=1), 0.0)
        a_prev = jnp.where(keep, pltpu.roll(a3, shift, axis=1), 1.0)
        b3 = b3 + a3 * b_prev
        a3 = a3 * a_prev
        shift *= 2
    h = h_ref[...]
    for g in range(groups):
        hg = a3[g] * h + b3[g]
        hs_ref[g * SUBLANES:(g + 1) * SUBLANES, :] = hg
        h = hg[SUBLANES - 1:SUBLANES, :]
    h_ref[...] = h

    o = hs_ref[...]
    y = ((o * _rms_scale(o)) * nl_ref[...]) * jax.nn.silu(gr_ref[...])
    y_ref[...] = y.astype(BF16)

    @pl.when(t == n_t - 1)
    def _():
        nconv_ref[...] = x[tt - (CONV_W - 1):tt, :]
        nlru_ref[...] = h


def _lru(x_r, g_r, conv_past, h_past, conv_w, conv_b, w_gates, b_gate_x, b_gate_a, lru_param, norm_lru, tt):
    b, t_len, _ = x_r.shape
    assert t_len % tt == 0 and tt % SUBLANES == 0 and tt >= SUBLANES
    seq_spec = pl.BlockSpec((None, tt, W_R), lambda i, j: (i, j, 0))
    vec = pl.BlockSpec((1, W_R), lambda i, j: (0, 0))
    conv_spec = pl.BlockSpec((None, CONV_W - 1, W_R), lambda i, j: (i, 0, 0))
    h_spec = pl.BlockSpec((None, 1, W_R), lambda i, j: (i, 0, 0))
    return pl.pallas_call(
        functools.partial(_lru_body, tt=tt),
        grid=(b, t_len // tt),
        in_specs=[seq_spec, seq_spec, conv_spec, h_spec,
                  pl.BlockSpec((CONV_W, W_R), lambda i, j: (0, 0)), vec,
                  pl.BlockSpec((W_R // MXU_TILE, MXU_TILE, 2 * MXU_TILE), lambda i, j: (0, 0, 0)),
                  vec, vec, vec, vec],
        out_specs=[seq_spec, conv_spec, h_spec],
        out_shape=[jax.ShapeDtypeStruct((b, t_len, W_R), BF16),
                   jax.ShapeDtypeStruct((b, CONV_W - 1, W_R), F32),
                   jax.ShapeDtypeStruct((b, 1, W_R), F32)],
        scratch_shapes=[pltpu.VMEM((SUBLANES, W_R), F32), pltpu.VMEM((1, W_R), F32), pltpu.VMEM((tt, W_R), F32)],
        compiler_params=_params(2),
        name="rglru",
    )(x_r, g_r, conv_past, h_past, conv_w, conv_b, w_gates, b_gate_x, b_gate_a, lru_param, norm_lru)


def _gate_weight_tiles(w_gate_x, w_gate_a):
    per_tile = MXU_TILE // LRU_BLOCK
    eye = jnp.eye(per_tile, dtype=F32)

    def tiles(w):
        w = w.reshape(N_LRU_BLOCKS // per_tile, per_tile, LRU_BLOCK, LRU_BLOCK)
        bd = jnp.einsum('jaik,ab->jaibk', w, eye)
        return bd.reshape(N_LRU_BLOCKS // per_tile, MXU_TILE, MXU_TILE)
    return jnp.concatenate([tiles(w_gate_x), tiles(w_gate_a)], axis=-1).astype(BF16)


def _outproj_body(o_ref, ga_ref, yl_ref, x_ref, na_ref, nf_ref, w_ref, y_ref):
    o = o_ref[...]
    ya = ((o * _rms_scale(o)) * na_ref[...]) * jax.nn.silu(ga_ref[...])
    acc = jnp.dot(ya.astype(BF16), w_ref[0:W_A, :], preferred_element_type=F32)
    acc = acc + jnp.dot(yl_ref[...], w_ref[W_A:W_A + W_R, :], preferred_element_type=F32)
    y = x_ref[...] + acc
    y_ref[...] = (y * _rms_scale(y)) * nf_ref[...]


def _outproj(o_att, g_a, y_lru, x2d, norm_attn, norm_final, w_out_bf16, tm):
    m = x2d.shape[0]
    row = pl.BlockSpec((tm, D_MODEL), lambda i: (i, 0))
    vec = pl.BlockSpec((1, D_MODEL), lambda i: (0, 0))
    return pl.pallas_call(
        _outproj_body,
        grid=(m // tm,),
        in_specs=[row, row, row, row, vec, vec, pl.BlockSpec((W_A + W_R, D_MODEL), lambda i: (0, 0))],
        out_specs=row,
        out_shape=jax.ShapeDtypeStruct((m, D_MODEL), F32),
        compiler_params=_params(1),
        name="outproj",
    )(o_att, g_a, y_lru, x2d, norm_attn, norm_final, w_out_bf16)


PROMPT_ROWS = 256
OUT_ROWS = 512
LRU_TILE = 256


def kernel(x_prompt, x_sample, cache_win_k, cache_win_v, state_conv, state_lru, rel_bias, norm_in, w_in, norm_attn, norm_lru, conv_w, conv_b, w_gate_x, b_gate_x, w_gate_a, b_gate_a, lru_param, w_out, norm_final):
    depth = w_in.shape[0]
    assert depth == 1, "single-layer trunk"
    bp, seq, _ = x_prompt.shape
    bs, n_new, _ = x_sample.shape
    wb = cache_win_k.shape[2]
    row = lambda p: p.reshape(1, -1)

    w_in_b = w_in[0].astype(BF16)
    w_out_b = w_out[0].astype(BF16)
    w_gates = _gate_weight_tiles(w_gate_x[0], w_gate_a[0])
    lru_args = (conv_w[0], row(conv_b[0]), w_gates, row(b_gate_x[0]), row(b_gate_a[0]), row(lru_param[0]), row(norm_lru[0]))

    xp = x_prompt.reshape(bp * seq, D_MODEL)
    q, k, v, g_a, x_r, g_r = _inproj(xp, row(norm_in[0]), w_in_b, PROMPT_ROWS)
    as_seq = lambda a: a.reshape(bp, seq, -1)
    o_att = _attn_prompt(as_seq(q), as_seq(k), as_seq(v), _prompt_bias_tables(rel_bias))
    y_lru, conv_p, lru_p = _lru(as_seq(x_r), as_seq(g_r),
                                jnp.zeros((bp, CONV_W - 1, W_R), F32), jnp.zeros((bp, 1, W_R), state_lru.dtype),
                                *lru_args, tt=LRU_TILE)
    y_prompt = _outproj(o_att.reshape(bp * seq, W_A), g_a, y_lru.reshape(bp * seq, W_R), xp,
                        row(norm_attn[0]), row(norm_final), w_out_b, OUT_ROWS)

    xs = x_sample.reshape(bs * n_new, D_MODEL)
    qs, ks, vs, gas, xrs, grs = _inproj(xs, row(norm_in[0]), w_in_b, bs * n_new)
    as_new = lambda a: a.reshape(bs, n_new, -1)
    n_ctx_pad = -(-(wb + n_new) // LANES) * LANES
    o_s, new_k, new_v = _attn_sample(as_new(qs), as_new(ks), as_new(vs),
                                     cache_win_k[0].reshape(bs, wb, W_A), cache_win_v[0].reshape(bs, wb, W_A),
                                     _sample_bias_tables(rel_bias, n_new, wb, n_ctx_pad))
    y_lru_s, conv_s, lru_s = _lru(as_new(xrs), as_new(grs), state_conv[0], state_lru[0].reshape(bs, 1, W_R),
                                  *lru_args, tt=n_new)
    y_sample = _outproj(o_s.reshape(bs * n_new, W_A), gas, y_lru_s.reshape(bs * n_new, W_R), xs,
                        row(norm_attn[0]), row(norm_final), w_out_b, bs * n_new)

    heads = lambda a, b: a.reshape(1, b, -1, N_HEADS, HEAD_DIM)
    return (y_prompt.reshape(bp, seq, D_MODEL), y_sample.reshape(bs, n_new, D_MODEL),
            heads(k, bp), heads(v, bp), conv_p[None], lru_p.reshape(1, bp, W_R),
            heads(new_k, bs), heads(new_v, bs), conv_s[None], lru_s.reshape(1, bs, W_R).astype(state_lru.dtype))
```

```python
import functools
import math

import numpy as np
import jax
import jax.numpy as jnp
from jax import lax
from jax.experimental import pallas as pl
from jax.experimental.pallas import tpu as pltpu

D_MODEL = 1024
HEAD_DIM = 64
N_HEADS = 16
W_A = N_HEADS * HEAD_DIM
W_R = D_MODEL
N_LRU_BLOCKS = 16
LRU_BLOCK = W_R // N_LRU_BLOCKS
N_PROJ = 6
CONV_W = 4
LRU_C = 8.0
DILATIONS = (1, 4, 16)
N_DIL = len(DILATIONS)
N_STEPS = 128
BLOCK = 128
N_BUCKETS = 32
MAX_EXACT = 16
MAX_DISTANCE = 2048
EPS = 1e-6
SCALE = HEAD_DIM ** -0.5
NEG_INF = -1e30

LANES = 128
SUBLANES = 8
HEADS_PER_PAIR = LANES // HEAD_DIM
N_PAIRS = N_HEADS // HEADS_PER_PAIR
MXU_TILE = 256
VMEM_LIMIT = 56 * 1024 * 1024

F32 = jnp.float32
BF16 = jnp.bfloat16
NT_DIMS = (((1,), (1,)), ((), ()))


def _params(n_axes):
    return pltpu.CompilerParams(dimension_semantics=("arbitrary",) * n_axes,
                                vmem_limit_bytes=VMEM_LIMIT)


def _rms_scale(x):
    return lax.rsqrt(jnp.mean(x * x, axis=-1, keepdims=True) + EPS)


def _resident(shape):
    return pl.BlockSpec(shape, lambda *_: (0,) * len(shape), pipeline_mode=pl.Buffered(1))


def _normed(x_ref, g_ref):
    x = x_ref[...]
    return ((x * _rms_scale(x)) * g_ref[...]).astype(BF16)


def _inproj_rows_body(x_ref, g_ref, w_ref, *out_refs):
    h = _normed(x_ref, g_ref)
    for i, o_ref in enumerate(out_refs):
        r = jnp.dot(h, w_ref[:, i * W_A:(i + 1) * W_A], preferred_element_type=F32)
        if i == 0:
            r = r * SCALE
        o_ref[...] = r


def _inproj_rows(x2d, norm_in, w_in_bf16):
    m = x2d.shape[0]
    row = pl.BlockSpec((m, D_MODEL), lambda i: (0, 0))
    return pl.pallas_call(
        _inproj_rows_body,
        grid=(1,),
        in_specs=[row, pl.BlockSpec((1, D_MODEL), lambda i: (0, 0)),
                  pl.BlockSpec((D_MODEL, N_PROJ * W_A), lambda i: (0, 0))],
        out_specs=[row] * N_PROJ,
        out_shape=[jax.ShapeDtypeStruct((m, W_A), F32)] * N_PROJ,
        compiler_params=_params(1),
        name="inproj_rows",
    )(x2d, norm_in, w_in_bf16)


def _inproj_prompt_body(x_ref, g_ref, wr_ref, wt_ref, q_ref, ga_ref, xr_ref, gr_ref, kt_ref, vt_ref):
    h = _normed(x_ref, g_ref)
    for i, o_ref in enumerate((q_ref, ga_ref, xr_ref, gr_ref)):
        r = jnp.dot(h, wr_ref[:, i * W_A:(i + 1) * W_A], preferred_element_type=F32)
        if i == 0:
            r = r * SCALE
        o_ref[...] = r
    for i, o_ref in enumerate((kt_ref, vt_ref)):
        o_ref[...] = lax.dot_general(wt_ref[i * W_A:(i + 1) * W_A, :], h, NT_DIMS, preferred_element_type=F32)


def _inproj_prompt(x, norm_in, w_rows, w_kv_t, tm):
    b, seq, _ = x.shape
    row = pl.BlockSpec((None, tm, D_MODEL), lambda i, j: (i, j, 0))
    col = pl.BlockSpec((None, W_A, tm), lambda i, j: (i, 0, j))
    rows_shape = jax.ShapeDtypeStruct((b, seq, W_A), F32)
    cols_shape = jax.ShapeDtypeStruct((b, W_A, seq), F32)
    return pl.pallas_call(
        _inproj_prompt_body,
        grid=(b, seq // tm),
        in_specs=[row, _resident((1, D_MODEL)), _resident(w_rows.shape), _resident(w_kv_t.shape)],
        out_specs=[row] * 4 + [col] * 2,
        out_shape=[rows_shape] * 4 + [cols_shape] * 2,
        compiler_params=_params(2),
        name="inproj_prompt",
    )(x, norm_in, w_rows, w_kv_t)


def _t5_bucket_np(dist):
    nf = np.maximum(dist, 1).astype(np.float32)
    large = MAX_EXACT + (np.log(nf / np.float32(MAX_EXACT)) / np.float32(math.log(MAX_DISTANCE / MAX_EXACT))
                         * np.float32(N_BUCKETS - MAX_EXACT)).astype(np.int32)
    large = np.minimum(large, N_BUCKETS - 1)
    return np.where(dist < MAX_EXACT, dist, large).astype(np.int32)


def _bucket_index_tables(n_new, wb):
    i = np.arange(BLOCK)[:, None]
    j = np.arange(2 * BLOCK)[None, :]
    diff = i - j + BLOCK
    band = (diff >= 0) & (diff <= N_STEPS)
    t = np.arange(n_new)[:, None]
    dist_old = wb + t - np.arange(wb)[None, :]
    dist_new = t - np.arange(LANES)[None, :]
    prompt, s_old, s_new = [], [], []
    for dil in DILATIONS:
        prompt.append(np.where(band, _t5_bucket_np(np.maximum(diff, 0) * dil), -1))
        for dist, out, extra in ((dist_old, s_old, True), (dist_new, s_new, np.arange(LANES)[None, :] < n_new)):
            ok = (dist >= 0) & (dist % dil == 0) & (dist // dil <= N_STEPS) & extra
            out.append(np.where(ok, _t5_bucket_np(np.maximum(dist, 0)), -1))
    as_i32 = lambda a: np.stack(a).astype(np.int32)
    return as_i32(prompt), as_i32(s_old), as_i32(s_new)


def _bias_tables_body(rb_ref, *refs, buckets):
    idx_refs, out_refs = refs[:3], refs[3:]
    for d in range(N_DIL):
        idx = [r[d] for r in idx_refs]

        def per_head(h, carry):
            tabs = [jnp.full(ix.shape, NEG_INF, F32) for ix in idx]
            for b in buckets[d]:
                val = rb_ref[b, h]
                tabs = [jnp.where(ix == b, val, tb) for ix, tb in zip(idx, tabs)]
            for o_ref, tb in zip(out_refs, tabs):
                o_ref[d, h] = tb
            return carry
        lax.fori_loop(0, N_HEADS, per_head, 0)


def _bias_tables(rel_bias, n_new, wb):
    idx = _bucket_index_tables(n_new, wb)
    buckets = tuple(tuple(int(b) for b in np.unique(np.concatenate([a[d].ravel() for a in idx])) if b >= 0)
                    for d in range(N_DIL))
    vmem = pl.BlockSpec(memory_space=pltpu.VMEM)
    return pl.pallas_call(
        functools.partial(_bias_tables_body, buckets=buckets),
        in_specs=[pl.BlockSpec(memory_space=pltpu.SMEM)] + [vmem] * 3,
        out_specs=[vmem] * 3,
        out_shape=[jax.ShapeDtypeStruct((N_DIL, N_HEADS) + a.shape[1:], F32) for a in idx],
        compiler_params=pltpu.CompilerParams(vmem_limit_bytes=VMEM_LIMIT),
        name="bias_tables",
    )(rel_bias, *[jnp.asarray(a) for a in idx])


BLOCKS_PER_TRIP = 8
INTERLEAVE = 4


def _attn_prompt_body(q_ref, kt_ref, vt_ref, bm_ref, o_ref, k_ref, v_ref, qg_ref, kg_ref, vg_ref,
                      acc16_ref, m16_ref, l16_ref, accp_ref, mp_ref, lp_ref, *, seq):
    group = seq // INTERLEAVE
    nb4 = group // BLOCK
    assert DILATIONS == (1, INTERLEAVE, INTERLEAVE * INTERLEAVE) and seq == DILATIONS[2] * BLOCK
    assert (INTERLEAVE * nb4) % BLOCKS_PER_TRIP == 0 and BLOCKS_PER_TRIP % nb4 == 0

    for t_ref, r_ref in ((kt_ref, k_ref), (vt_ref, v_ref)):
        r_ref[0:BLOCK, :] = jnp.zeros((BLOCK, LANES), F32)
        r_ref[BLOCK:BLOCK + seq, :] = t_ref[...].T
    for src, dst, off in ((q_ref, qg_ref, 0), (k_ref, kg_ref, BLOCK), (v_ref, vg_ref, BLOCK)):
        for g in range(INTERLEAVE):
            dst[g * group:(g + 1) * group, :] = src[pl.ds(off + g, group, stride=INTERLEAVE), :]

    first = lax.broadcasted_iota(jnp.int32, (BLOCK, LANES), 1) < HEAD_DIM
    prev_cols = lax.broadcasted_iota(jnp.int32, (HEADS_PER_PAIR * BLOCK, 2 * BLOCK), 1) < BLOCK

    def rows_of(start, n, stride):
        return pl.ds(start, n) if stride == 1 else pl.ds(start, n, stride=stride)

    def block(dil_idx, refs, stride, q_start, k_start, with_prev, extra_bias=None):
        qs_ref, ks_ref, vs_ref = refs
        q = qs_ref[rows_of(q_start, BLOCK, stride), :]
        q2 = jnp.concatenate([jnp.where(first, q, 0.0), jnp.where(first, 0.0, q)], axis=0).astype(BF16)
        nk = 2 * BLOCK if with_prev else BLOCK
        kb = ks_ref[rows_of(k_start, nk, stride), :].astype(BF16)
        vb = vs_ref[rows_of(k_start, nk, stride), :].astype(BF16)
        bias = bm_ref[dil_idx] if with_prev else bm_ref[dil_idx, :, BLOCK:]
        s = lax.dot_general(q2, kb, NT_DIMS, preferred_element_type=F32) + bias
        if extra_bias is not None:
            s = s + extra_bias
        m = jnp.max(s, axis=-1, keepdims=True)
        e = jnp.exp(s - m).astype(BF16)
        v1 = jnp.concatenate([vb, jnp.ones((nk, LANES), BF16)], axis=1)
        acc = jnp.dot(e, v1, preferred_element_type=F32)
        pick = lambda a: jnp.where(first, a[:BLOCK], a[BLOCK:])
        return pick(acc[:, :LANES]), pick(m), pick(acc[:, LANES:])

    def merge(parts):
        mx = functools.reduce(jnp.maximum, [m for _, m, _ in parts])
        scale = [jnp.exp(m - mx) for _, m, _ in parts]
        acc = functools.reduce(lambda x, y: x + y, [w * a for w, (a, _, _) in zip(scale, parts)])
        den = functools.reduce(lambda x, y: x + y, [w * l for w, (_, _, l) in zip(scale, parts)])
        return acc, mx, den

    grouped = (qg_ref, kg_ref, vg_ref)
    n_blocks = INTERLEAVE * nb4

    def d16_trip(it, carry):
        for u in range(BLOCKS_PER_TRIP):
            idx = it * BLOCKS_PER_TRIP + u
            start = (idx // INTERLEAVE) * group + idx % INTERLEAVE
            vals = block(2, grouped, INTERLEAVE, start, start, False)
            for ref, val in zip((acc16_ref, m16_ref, l16_ref), vals):
                ref[pl.ds(start, BLOCK, stride=INTERLEAVE), :] = val
        return carry
    lax.fori_loop(0, n_blocks // BLOCKS_PER_TRIP, d16_trip, 0)

    def d4_trip(it, carry):
        for u in range(BLOCKS_PER_TRIP):
            g = it * (BLOCKS_PER_TRIP // nb4) + u // nb4
            n = u % nb4
            start = pl.multiple_of(g * group + n * BLOCK, BLOCK)
            k_start = start - BLOCK if n > 0 else start
            rows = pl.ds(start, BLOCK)
            vals = merge([block(1, grouped, 1, start, k_start, n > 0),
                          (acc16_ref[rows, :], m16_ref[rows, :], l16_ref[rows, :])])
            for ref, val in zip((accp_ref, mp_ref, lp_ref), vals):
                ref[pl.ds(g + n * BLOCK * INTERLEAVE, BLOCK, stride=INTERLEAVE), :] = val
        return carry
    lax.fori_loop(0, n_blocks // BLOCKS_PER_TRIP, d4_trip, 0)

    natural = (q_ref, k_ref, v_ref)
    nb1 = seq // BLOCK
    assert nb1 % BLOCKS_PER_TRIP == 0

    def d1_trip(it, carry):
        for u in range(BLOCKS_PER_TRIP):
            n = it * BLOCKS_PER_TRIP + u
            start = pl.multiple_of(n * BLOCK, BLOCK)
            extra = None
            if u == 0:
                no_prev = jnp.where(it == 0, NEG_INF, 0.0).astype(F32)
                extra = jnp.where(prev_cols, no_prev, 0.0)
            rows = pl.ds(start, BLOCK)
            acc, _, den = merge([block(0, natural, 1, start, start, True, extra),
                                 (accp_ref[rows, :], mp_ref[rows, :], lp_ref[rows, :])])
            o_ref[rows, :] = acc / den
        return carry
    lax.fori_loop(0, nb1 // BLOCKS_PER_TRIP, d1_trip, 0)


def _attn_prompt(q, k_t, v_t, bias_tabs):
    b, seq, _ = q.shape
    rows = pl.BlockSpec((None, seq, LANES), lambda p, i: (i, 0, p))
    cols = pl.BlockSpec((None, LANES, seq), lambda p, i: (i, p, 0))
    scratch = [pltpu.VMEM((seq + BLOCK, LANES), F32)] * 2 + [pltpu.VMEM((seq, LANES), F32)] * 9
    return pl.pallas_call(
        functools.partial(_attn_prompt_body, seq=seq),
        grid=(N_PAIRS, b),
        in_specs=[rows, cols, cols,
                  pl.BlockSpec((N_DIL, None, HEADS_PER_PAIR * BLOCK, 2 * BLOCK), lambda p, i: (0, p, 0, 0))],
        out_specs=rows,
        out_shape=jax.ShapeDtypeStruct((b, seq, W_A), F32),
        scratch_shapes=scratch,
        compiler_params=_params(2),
        name="attn_prompt",
    )(q, k_t, v_t, bias_tabs)


SAMPLE_HEADS = 8
SAMPLE_LANES = SAMPLE_HEADS * HEAD_DIM


def _attn_sample_body(q_ref, kn_ref, vn_ref, ckt_ref, cvt_ref, bo_ref, bn_ref, o_ref, nkt_ref, nvt_ref,
                      *, wb, n_new):
    rows = SAMPLE_HEADS * n_new
    row_head = lax.broadcasted_iota(jnp.int32, (rows, SAMPLE_LANES), 0) // n_new
    col_head = lax.broadcasted_iota(jnp.int32, (rows, SAMPLE_LANES), 1) // HEAD_DIM
    own_head = row_head == col_head
    q_rep = jnp.concatenate([q_ref[...]] * SAMPLE_HEADS, axis=0)
    q_bd = jnp.where(own_head, q_rep, 0.0).astype(BF16)

    pad = jnp.zeros((LANES - n_new, SAMPLE_LANES), F32)
    kn = jnp.concatenate([kn_ref[...], pad], axis=0)
    vn = jnp.concatenate([vn_ref[...], pad], axis=0)
    kt = ckt_ref[...]
    vt = cvt_ref[...]

    s_old = jnp.dot(q_bd, kt.astype(BF16), preferred_element_type=F32)
    s_new = lax.dot_general(q_bd, kn.astype(BF16), NT_DIMS, preferred_element_type=F32)
    lo = [s_old + bo_ref[d] for d in range(N_DIL)]
    ln = [s_new + bn_ref[d] for d in range(N_DIL)]
    row_max = lambda t: jnp.max(t, axis=-1, keepdims=True)
    mx = functools.reduce(jnp.maximum, [row_max(t) for t in lo + ln])
    e_old = functools.reduce(lambda a, b: a + b, [jnp.exp(t - mx) for t in lo])
    e_new = functools.reduce(lambda a, b: a + b, [jnp.exp(t - mx) for t in ln])
    den = jnp.sum(e_old, axis=-1, keepdims=True) + jnp.sum(e_new, axis=-1, keepdims=True)
    o_full = lax.dot_general(e_old.astype(BF16), vt.astype(BF16), NT_DIMS, preferred_element_type=F32)
    o_full = o_full + jnp.dot(e_new.astype(BF16), vn.astype(BF16), preferred_element_type=F32)
    o_full = jnp.where(own_head, o_full / den, 0.0)
    out = o_full[0:n_new, :]
    for h in range(1, SAMPLE_HEADS):
        out = out + o_full[h * n_new:(h + 1) * n_new, :]
    o_ref[...] = out

    last = lax.broadcasted_iota(jnp.int32, (SAMPLE_LANES, LANES), 1) >= LANES - n_new
    for c_t, new, out_ref in ((kt, kn, nkt_ref), (vt, vn, nvt_ref)):
        rolled = pltpu.roll(c_t, wb - n_new, axis=1)
        out_ref[...] = rolled
        tail = pltpu.roll(new.T, LANES - n_new, axis=1)
        out_ref[:, wb - LANES:wb] = jnp.where(last, tail, rolled[:, wb - LANES:wb])


def _attn_sample(q, k_new, v_new, cache_kt, cache_vt, bias_old, bias_new):
    b, n_new, _ = q.shape
    wb = cache_kt.shape[2]
    n_groups = N_HEADS // SAMPLE_HEADS
    new_spec = pl.BlockSpec((None, n_new, SAMPLE_LANES), lambda i, g: (i, 0, g))
    cache_spec = pl.BlockSpec((None, SAMPLE_LANES, wb), lambda i, g: (i, g, 0))
    rows = SAMPLE_HEADS * n_new
    return pl.pallas_call(
        functools.partial(_attn_sample_body, wb=wb, n_new=n_new),
        grid=(b, n_groups),
        in_specs=[new_spec, new_spec, new_spec, cache_spec, cache_spec,
                  pl.BlockSpec((N_DIL, rows, wb), lambda i, g: (0, g, 0)),
                  pl.BlockSpec((N_DIL, rows, LANES), lambda i, g: (0, g, 0))],
        out_specs=[new_spec, cache_spec, cache_spec],
        out_shape=[jax.ShapeDtypeStruct((b, n_new, W_A), F32),
                   jax.ShapeDtypeStruct((b, W_A, wb), F32),
                   jax.ShapeDtypeStruct((b, W_A, wb), F32)],
        compiler_params=_params(2),
        name="attn_sample",
    )(q, k_new, v_new, cache_kt, cache_vt, bias_old, bias_new)


def _lru_body(xr_ref, gr_ref, cpast_ref, hpast_ref, cw_ref, cb_ref, wg_ref, bgx_ref, bga_ref, lp_ref, nl_ref,
              y_ref, nconv_ref, nlru_ref, tail_ref, h_ref, hs_ref, *, tt):
    t = pl.program_id(1)
    n_t = pl.num_programs(1)

    @pl.when(t == 0)
    def _():
        tail_ref[...] = jnp.concatenate(
            [jnp.zeros((SUBLANES - (CONV_W - 1), W_R), F32), cpast_ref[...]], axis=0)
        h_ref[...] = hpast_ref[...]

    x = xr_ref[...]
    xe = jnp.concatenate([tail_ref[...], x], axis=0)
    xc = cb_ref[...] + cw_ref[CONV_W - 1:CONV_W, :] * x
    for tap in range(CONV_W - 1):
        back = CONV_W - 1 - tap
        xc = xc + cw_ref[tap:tap + 1, :] * xe[SUBLANES - back:SUBLANES - back + tt, :]
    tail_ref[...] = x[tt - SUBLANES:tt, :]

    xcb = xc.astype(BF16)
    gx, ga = [], []
    for j in range(W_R // MXU_TILE):
        g = jnp.dot(xcb[:, j * MXU_TILE:(j + 1) * MXU_TILE], wg_ref[j], preferred_element_type=F32)
        gx.append(g[:, :MXU_TILE])
        ga.append(g[:, MXU_TILE:])
    gate_x = jax.nn.sigmoid(jnp.concatenate(gx, axis=-1) + bgx_ref[...])
    gate_a = jax.nn.sigmoid(jnp.concatenate(ga, axis=-1) + bga_ref[...])
    log_a = (-LRU_C) * gate_a * jax.nn.softplus(-lp_ref[...])
    a = jnp.exp(log_a)
    bx = jnp.sqrt(jnp.tanh(-log_a) * (1.0 + a * a)) * (gate_x * xc)

    groups = tt // SUBLANES
    a3 = a.reshape(groups, SUBLANES, W_R)
    b3 = bx.reshape(groups, SUBLANES, W_R)
    sub = lax.broadcasted_iota(jnp.int32, (groups, SUBLANES, W_R), 1)
    shift = 1
    while shift < SUBLANES:
        keep = sub >= shift
        b_prev = jnp.where(keep, pltpu.roll(b3, shift, axis=1), 0.0)
        a_prev = jnp.where(keep, pltpu.roll(a3, shift, axis=1), 1.0)
        b3 = b3 + a3 * b_prev
        a3 = a3 * a_prev
        shift *= 2
    h = h_ref[...]
    for g in range(groups):
        hg = a3[g] * h + b3[g]
        hs_ref[g * SUBLANES:(g + 1) * SUBLANES, :] = hg
        h = hg[SUBLANES - 1:SUBLANES, :]
    h_ref[...] = h

    o = hs_ref[...]
    y = ((o * _rms_scale(o)) * nl_ref[...]) * jax.nn.silu(gr_ref[...])
    y_ref[...] = y.astype(BF16)

    @pl.when(t == n_t - 1)
    def _():
        nconv_ref[...] = x[tt - (CONV_W - 1):tt, :]
        nlru_ref[...] = h


def _lru(x_r, g_r, conv_past, h_past, conv_w, conv_b, w_gates, b_gate_x, b_gate_a, lru_param, norm_lru, tt):
    b, t_len, _ = x_r.shape
    assert t_len % tt == 0 and tt % SUBLANES == 0 and tt >= SUBLANES
    seq_spec = pl.BlockSpec((None, tt, W_R), lambda i, j: (i, j, 0))
    vec = pl.BlockSpec((1, W_R), lambda i, j: (0, 0))
    conv_spec = pl.BlockSpec((None, CONV_W - 1, W_R), lambda i, j: (i, 0, 0))
    h_spec = pl.BlockSpec((None, 1, W_R), lambda i, j: (i, 0, 0))
    return pl.pallas_call(
        functools.partial(_lru_body, tt=tt),
        grid=(b, t_len // tt),
        in_specs=[seq_spec, seq_spec, conv_spec, h_spec,
                  pl.BlockSpec((CONV_W, W_R), lambda i, j: (0, 0)), vec,
                  pl.BlockSpec((W_R // MXU_TILE, MXU_TILE, 2 * MXU_TILE), lambda i, j: (0, 0, 0)),
                  vec, vec, vec, vec],
        out_specs=[seq_spec, conv_spec, h_spec],
        out_shape=[jax.ShapeDtypeStruct((b, t_len, W_R), BF16),
                   jax.ShapeDtypeStruct((b, CONV_W - 1, W_R), F32),
                   jax.ShapeDtypeStruct((b, 1, W_R), F32)],
        scratch_shapes=[pltpu.VMEM((SUBLANES, W_R), F32), pltpu.VMEM((1, W_R), F32), pltpu.VMEM((tt, W_R), F32)],
        compiler_params=_params(2),
        name="rglru",
    )(x_r, g_r, conv_past, h_past, conv_w, conv_b, w_gates, b_gate_x, b_gate_a, lru_param, norm_lru)


def _gate_weight_tiles(w_gate_x, w_gate_a):
    per_tile = MXU_TILE // LRU_BLOCK
    eye = jnp.eye(per_tile, dtype=F32)

    def tiles(w):
        w = w.reshape(N_LRU_BLOCKS // per_tile, per_tile, LRU_BLOCK, LRU_BLOCK)
        bd = jnp.einsum('jaik,ab->jaibk', w, eye)
        return bd.reshape(N_LRU_BLOCKS // per_tile, MXU_TILE, MXU_TILE)
    return jnp.concatenate([tiles(w_gate_x), tiles(w_gate_a)], axis=-1).astype(BF16)


def _outproj_body(o_ref, ga_ref, yl_ref, x_ref, na_ref, nf_ref, w_ref, y_ref):
    o = o_ref[...]
    ya = ((o * _rms_scale(o)) * na_ref[...]) * jax.nn.silu(ga_ref[...])
    acc = jnp.dot(ya.astype(BF16), w_ref[0:W_A, :], preferred_element_type=F32)
    acc = acc + jnp.dot(yl_ref[...], w_ref[W_A:W_A + W_R, :], preferred_element_type=F32)
    y = x_ref[...] + acc
    y_ref[...] = (y * _rms_scale(y)) * nf_ref[...]


def _outproj(o_att, g_a, y_lru, x2d, norm_attn, norm_final, w_out_bf16, tm):
    m = x2d.shape[0]
    row = pl.BlockSpec((tm, D_MODEL), lambda i: (i, 0))
    vec = pl.BlockSpec((1, D_MODEL), lambda i: (0, 0))
    return pl.pallas_call(
        _outproj_body,
        grid=(m // tm,),
        in_specs=[row, row, row, row, vec, vec, pl.BlockSpec((W_A + W_R, D_MODEL), lambda i: (0, 0))],
        out_specs=row,
        out_shape=jax.ShapeDtypeStruct((m, D_MODEL), F32),
        compiler_params=_params(1),
        name="outproj",
    )(o_att, g_a, y_lru, x2d, norm_attn, norm_final, w_out_bf16)


PROMPT_ROWS = 512
OUT_ROWS = 512
LRU_TILE = 256


def _to_head_major(win):
    b, p = win.shape[0], win.shape[1]
    return jnp.transpose(win, (0, 2, 3, 1)).reshape(b, W_A, p)


def _from_head_major(win_t):
    b, _, p = win_t.shape
    return jnp.transpose(win_t.reshape(b, N_HEADS, HEAD_DIM, p), (0, 3, 1, 2))[None]


def kernel(x_prompt, x_sample, cache_win_k, cache_win_v, state_conv, state_lru, rel_bias, norm_in, w_in, norm_attn, norm_lru, conv_w, conv_b, w_gate_x, b_gate_x, w_gate_a, b_gate_a, lru_param, w_out, norm_final):
    depth = w_in.shape[0]
    assert depth == 1, "single-layer trunk"
    bp, seq, _ = x_prompt.shape
    bs, n_new, _ = x_sample.shape
    wb = cache_win_k.shape[2]
    row = lambda p: p.reshape(1, -1)

    w_in_b = w_in[0].astype(BF16)
    group = lambda i: w_in_b[:, i * W_A:(i + 1) * W_A]
    w_rows = jnp.concatenate([group(0), group(3), group(4), group(5)], axis=1)
    w_kv_t = jnp.concatenate([group(1).T, group(2).T], axis=0)
    w_out_b = w_out[0].astype(BF16)
    w_gates = _gate_weight_tiles(w_gate_x[0], w_gate_a[0])
    lru_args = (conv_w[0], row(conv_b[0]), w_gates, row(b_gate_x[0]), row(b_gate_a[0]), row(lru_param[0]), row(norm_lru[0]))
    bias_prompt, bias_old, bias_new = _bias_tables(rel_bias, n_new, wb)

    q, g_a, x_r, g_r, k_t, v_t = _inproj_prompt(x_prompt, row(norm_in[0]), w_rows, w_kv_t, PROMPT_ROWS)
    o_att = _attn_prompt(q, k_t, v_t, bias_prompt.reshape(N_DIL, N_PAIRS, HEADS_PER_PAIR * BLOCK, 2 * BLOCK))
    y_lru, conv_p, lru_p = _lru(x_r, g_r,
                                jnp.zeros((bp, CONV_W - 1, W_R), F32), jnp.zeros((bp, 1, W_R), state_lru.dtype),
                                *lru_args, tt=LRU_TILE)
    flat = lambda a: a.reshape(bp * seq, -1)
    y_prompt = _outproj(flat(o_att), flat(g_a), flat(y_lru), flat(x_prompt),
                        row(norm_attn[0]), row(norm_final), w_out_b, OUT_ROWS)

    xs = x_sample.reshape(bs * n_new, D_MODEL)
    qs, ks, vs, gas, xrs, grs = _inproj_rows(xs, row(norm_in[0]), w_in_b)
    as_new = lambda a: a.reshape(bs, n_new, -1)
    o_s, new_kt, new_vt = _attn_sample(as_new(qs), as_new(ks), as_new(vs),
                                       _to_head_major(cache_win_k[0]), _to_head_major(cache_win_v[0]),
                                       bias_old.reshape(N_DIL, N_HEADS * n_new, wb),
                                       bias_new.reshape(N_DIL, N_HEADS * n_new, LANES))
    y_lru_s, conv_s, lru_s = _lru(as_new(xrs), as_new(grs), state_conv[0], state_lru[0].reshape(bs, 1, W_R),
                                  *lru_args, tt=n_new)
    y_sample = _outproj(o_s.reshape(bs * n_new, W_A), gas, y_lru_s.reshape(bs * n_new, W_R), xs,
                        row(norm_attn[0]), row(norm_final), w_out_b, bs * n_new)

    return (y_prompt.reshape(bp, seq, D_MODEL), y_sample.reshape(bs, n_new, D_MODEL),
            _from_head_major(k_t), _from_head_major(v_t), conv_p[None], lru_p.reshape(1, bp, W_R),
            _from_head_major(new_kt), _from_head_major(new_vt), conv_s[None],
            lru_s.reshape(1, bs, W_R).astype(state_lru.dtype))
```

```python
import functools
import math

import numpy as np
import jax
import jax.numpy as jnp
from jax import lax
from jax.experimental import pallas as pl
from jax.experimental.pallas import tpu as pltpu

D_MODEL = 1024
HEAD_DIM = 64
N_HEADS = 16
W_A = N_HEADS * HEAD_DIM
W_R = D_MODEL
N_LRU_BLOCKS = 16
LRU_BLOCK = W_R // N_LRU_BLOCKS
N_PROJ = 6
CONV_W = 4
LRU_C = 8.0
DILATIONS = (1, 4, 16)
N_DIL = len(DILATIONS)
N_STEPS = 128
BLOCK = 128
N_BUCKETS = 32
MAX_EXACT = 16
MAX_DISTANCE = 2048
EPS = 1e-6
SCALE = HEAD_DIM ** -0.5
NEG_INF = -1e30

LANES = 128
SUBLANES = 8
HEADS_PER_PAIR = LANES // HEAD_DIM
N_PAIRS = N_HEADS // HEADS_PER_PAIR
MXU_TILE = 256
VMEM_LIMIT = 56 * 1024 * 1024

F32 = jnp.float32
BF16 = jnp.bfloat16
NT_DIMS = (((1,), (1,)), ((), ()))


def _params(n_axes, vmem_limit=VMEM_LIMIT):
    return pltpu.CompilerParams(dimension_semantics=("arbitrary",) * n_axes,
                                vmem_limit_bytes=vmem_limit)


def _rms_scale(x):
    return lax.rsqrt(jnp.mean(x * x, axis=-1, keepdims=True) + EPS)


def _resident(shape):
    return pl.BlockSpec(shape, lambda *_: (0,) * len(shape), pipeline_mode=pl.Buffered(1))


def _normed(x_ref, g_ref):
    x = x_ref[...]
    return ((x * _rms_scale(x)) * g_ref[...]).astype(BF16)


def _inproj_rows_body(x_ref, g_ref, w_ref, *out_refs):
    h = _normed(x_ref, g_ref)
    for i, o_ref in enumerate(out_refs):
        r = jnp.dot(h, w_ref[:, i * W_A:(i + 1) * W_A], preferred_element_type=F32)
        if i == 0:
            r = r * SCALE
        o_ref[...] = r


def _inproj_rows(x2d, norm_in, w_in_bf16):
    m = x2d.shape[0]
    row = pl.BlockSpec((m, D_MODEL), lambda i: (0, 0))
    return pl.pallas_call(
        _inproj_rows_body,
        grid=(1,),
        in_specs=[row, pl.BlockSpec((1, D_MODEL), lambda i: (0, 0)),
                  pl.BlockSpec((D_MODEL, N_PROJ * W_A), lambda i: (0, 0))],
        out_specs=[row] * N_PROJ,
        out_shape=[jax.ShapeDtypeStruct((m, W_A), F32)] * N_PROJ,
        compiler_params=_params(1),
        name="inproj_rows",
    )(x2d, norm_in, w_in_bf16)


def _inproj_prompt_body(x_ref, g_ref, wr_ref, wt_ref, q_ref, ga_ref, xr_ref, gr_ref, kt_ref, vt_ref):
    h = _normed(x_ref, g_ref)
    for i, o_ref in enumerate((q_ref, ga_ref, xr_ref, gr_ref)):
        r = jnp.dot(h, wr_ref[:, i * W_A:(i + 1) * W_A], preferred_element_type=F32)
        if i == 0:
            r = r * SCALE
        o_ref[...] = r
    for i, o_ref in enumerate((kt_ref, vt_ref)):
        o_ref[...] = lax.dot_general(wt_ref[i * W_A:(i + 1) * W_A, :], h, NT_DIMS, preferred_element_type=F32)


def _inproj_prompt(x, norm_in, w_rows, w_kv_t, tm):
    b, seq, _ = x.shape
    row = pl.BlockSpec((None, tm, D_MODEL), lambda i, j: (i, j, 0))
    col = pl.BlockSpec((None, W_A, tm), lambda i, j: (i, 0, j))
    rows_shape = jax.ShapeDtypeStruct((b, seq, W_A), F32)
    cols_shape = jax.ShapeDtypeStruct((b, W_A, seq), F32)
    return pl.pallas_call(
        _inproj_prompt_body,
        grid=(b, seq // tm),
        in_specs=[row, _resident((1, D_MODEL)), _resident(w_rows.shape), _resident(w_kv_t.shape)],
        out_specs=[row] * 4 + [col] * 2,
        out_shape=[rows_shape] * 4 + [cols_shape] * 2,
        compiler_params=_params(2),
        name="inproj_prompt",
    )(x, norm_in, w_rows, w_kv_t)


def _t5_bucket_np(dist):
    nf = np.maximum(dist, 1).astype(np.float32)
    large = MAX_EXACT + (np.log(nf / np.float32(MAX_EXACT)) / np.float32(math.log(MAX_DISTANCE / MAX_EXACT))
                         * np.float32(N_BUCKETS - MAX_EXACT)).astype(np.int32)
    large = np.minimum(large, N_BUCKETS - 1)
    return np.where(dist < MAX_EXACT, dist, large).astype(np.int32)


def _bucket_index_tables(n_new, wb):
    i = np.arange(BLOCK)[:, None]
    j = np.arange(2 * BLOCK)[None, :]
    diff = i - j + BLOCK
    band = (diff >= 0) & (diff <= N_STEPS)
    t = np.arange(n_new)[:, None]
    dist_old = wb + t - np.arange(wb)[None, :]
    dist_new = t - np.arange(LANES)[None, :]
    prompt, s_old, s_new = [], [], []
    for dil in DILATIONS:
        prompt.append(np.where(band, _t5_bucket_np(np.maximum(diff, 0) * dil), -1))
        for dist, out, extra in ((dist_old, s_old, True), (dist_new, s_new, np.arange(LANES)[None, :] < n_new)):
            ok = (dist >= 0) & (dist % dil == 0) & (dist // dil <= N_STEPS) & extra
            out.append(np.where(ok, _t5_bucket_np(np.maximum(dist, 0)), -1))
    as_i32 = lambda a: np.stack(a).astype(np.int32)
    return as_i32(prompt), as_i32(s_old), as_i32(s_new)


def _bias_tables_body(rb_ref, *refs, buckets):
    idx_refs, out_refs = refs[:3], refs[3:]
    for d in range(N_DIL):
        idx = [r[d] for r in idx_refs]

        def per_head(h, carry):
            tabs = [jnp.full(ix.shape, NEG_INF, F32) for ix in idx]
            for b in buckets[d]:
                val = rb_ref[b, h]
                tabs = [jnp.where(ix == b, val, tb) for ix, tb in zip(idx, tabs)]
            for o_ref, tb in zip(out_refs, tabs):
                o_ref[d, h] = tb
            return carry
        lax.fori_loop(0, N_HEADS, per_head, 0)


def _bias_tables(rel_bias, n_new, wb):
    idx = _bucket_index_tables(n_new, wb)
    buckets = tuple(tuple(int(b) for b in np.unique(np.concatenate([a[d].ravel() for a in idx])) if b >= 0)
                    for d in range(N_DIL))
    vmem = pl.BlockSpec(memory_space=pltpu.VMEM)
    return pl.pallas_call(
        functools.partial(_bias_tables_body, buckets=buckets),
        in_specs=[pl.BlockSpec(memory_space=pltpu.SMEM)] + [vmem] * 3,
        out_specs=[vmem] * 3,
        out_shape=[jax.ShapeDtypeStruct((N_DIL, N_HEADS) + a.shape[1:], F32) for a in idx],
        compiler_params=pltpu.CompilerParams(vmem_limit_bytes=VMEM_LIMIT),
        name="bias_tables",
    )(rel_bias, *[jnp.asarray(a) for a in idx])


BLOCKS_PER_TRIP = 8
INTERLEAVE = 4


def _attn_prompt_body(q_ref, kt_ref, vt_ref, bm_ref, o_ref, k_ref, v_ref, qg_ref, kg_ref, vg_ref,
                      acc16_ref, m16_ref, l16_ref, accp_ref, mp_ref, lp_ref, *, seq):
    group = seq // INTERLEAVE
    nb4 = group // BLOCK
    assert DILATIONS == (1, INTERLEAVE, INTERLEAVE * INTERLEAVE) and seq == DILATIONS[2] * BLOCK
    assert (INTERLEAVE * nb4) % BLOCKS_PER_TRIP == 0 and BLOCKS_PER_TRIP % nb4 == 0

    for t_ref, r_ref in ((kt_ref, k_ref), (vt_ref, v_ref)):
        r_ref[0:BLOCK, :] = jnp.zeros((BLOCK, LANES), F32)
        r_ref[BLOCK:BLOCK + seq, :] = t_ref[...].T
    for src, dst, off in ((q_ref, qg_ref, 0), (k_ref, kg_ref, BLOCK), (v_ref, vg_ref, BLOCK)):
        for g in range(INTERLEAVE):
            dst[g * group:(g + 1) * group, :] = src[pl.ds(off + g, group, stride=INTERLEAVE), :]

    first = lax.broadcasted_iota(jnp.int32, (BLOCK, LANES), 1) < HEAD_DIM
    prev_cols = lax.broadcasted_iota(jnp.int32, (HEADS_PER_PAIR * BLOCK, 2 * BLOCK), 1) < BLOCK

    def rows_of(start, n, stride):
        return pl.ds(start, n) if stride == 1 else pl.ds(start, n, stride=stride)

    def block(dil_idx, refs, stride, q_start, k_start, with_prev, extra_bias=None):
        qs_ref, ks_ref, vs_ref = refs
        q = qs_ref[rows_of(q_start, BLOCK, stride), :]
        q2 = jnp.concatenate([jnp.where(first, q, 0.0), jnp.where(first, 0.0, q)], axis=0).astype(BF16)
        nk = 2 * BLOCK if with_prev else BLOCK
        kb = ks_ref[rows_of(k_start, nk, stride), :].astype(BF16)
        vb = vs_ref[rows_of(k_start, nk, stride), :].astype(BF16)
        bias = bm_ref[dil_idx] if with_prev else bm_ref[dil_idx, :, BLOCK:]
        s = lax.dot_general(q2, kb, NT_DIMS, preferred_element_type=F32) + bias
        if extra_bias is not None:
            s = s + extra_bias
        m = jnp.max(s, axis=-1, keepdims=True)
        e = jnp.exp(s - m).astype(BF16)
        v1 = jnp.concatenate([vb, jnp.ones((nk, LANES), BF16)], axis=1)
        acc = jnp.dot(e, v1, preferred_element_type=F32)
        pick = lambda a: jnp.where(first, a[:BLOCK], a[BLOCK:])
        return pick(acc[:, :LANES]), pick(m), pick(acc[:, LANES:])

    def merge(parts):
        mx = functools.reduce(jnp.maximum, [m for _, m, _ in parts])
        scale = [jnp.exp(m - mx) for _, m, _ in parts]
        acc = functools.reduce(lambda x, y: x + y, [w * a for w, (a, _, _) in zip(scale, parts)])
        den = functools.reduce(lambda x, y: x + y, [w * l for w, (_, _, l) in zip(scale, parts)])
        return acc, mx, den

    grouped = (qg_ref, kg_ref, vg_ref)
    n_blocks = INTERLEAVE * nb4

    for idx in range(n_blocks):
        start = (idx // INTERLEAVE) * group + idx % INTERLEAVE
        vals = block(2, grouped, INTERLEAVE, start, start, False)
        for ref, val in zip((acc16_ref, m16_ref, l16_ref), vals):
            ref[pl.ds(start, BLOCK, stride=INTERLEAVE), :] = val

    def d4_trip(it, carry):
        for u in range(BLOCKS_PER_TRIP):
            g = it * (BLOCKS_PER_TRIP // nb4) + u // nb4
            n = u % nb4
            start = pl.multiple_of(g * group + n * BLOCK, BLOCK)
            k_start = start - BLOCK if n > 0 else start
            rows = pl.ds(start, BLOCK)
            vals = merge([block(1, grouped, 1, start, k_start, n > 0),
                          (acc16_ref[rows, :], m16_ref[rows, :], l16_ref[rows, :])])
            for ref, val in zip((accp_ref, mp_ref, lp_ref), vals):
                ref[pl.ds(g + n * BLOCK * INTERLEAVE, BLOCK, stride=INTERLEAVE), :] = val
        return carry
    lax.fori_loop(0, n_blocks // BLOCKS_PER_TRIP, d4_trip, 0)

    natural = (q_ref, k_ref, v_ref)
    nb1 = seq // BLOCK
    assert nb1 % BLOCKS_PER_TRIP == 0

    def d1_trip(it, carry):
        for u in range(BLOCKS_PER_TRIP):
            n = it * BLOCKS_PER_TRIP + u
            start = pl.multiple_of(n * BLOCK, BLOCK)
            extra = None
            if u == 0:
                no_prev = jnp.where(it == 0, NEG_INF, 0.0).astype(F32)
                extra = jnp.where(prev_cols, no_prev, 0.0)
            rows = pl.ds(start, BLOCK)
            acc, _, den = merge([block(0, natural, 1, start, start, True, extra),
                                 (accp_ref[rows, :], mp_ref[rows, :], lp_ref[rows, :])])
            o_ref[rows, :] = acc / den
        return carry
    lax.fori_loop(0, nb1 // BLOCKS_PER_TRIP, d1_trip, 0)


def _attn_prompt(q, k_t, v_t, bias_tabs):
    b, seq, _ = q.shape
    rows = pl.BlockSpec((None, seq, LANES), lambda p, i: (i, 0, p))
    cols = pl.BlockSpec((None, LANES, seq), lambda p, i: (i, p, 0))
    scratch = [pltpu.VMEM((seq + BLOCK, LANES), F32)] * 2 + [pltpu.VMEM((seq, LANES), F32)] * 9
    return pl.pallas_call(
        functools.partial(_attn_prompt_body, seq=seq),
        grid=(N_PAIRS, b),
        in_specs=[rows, cols, cols,
                  pl.BlockSpec((N_DIL, None, HEADS_PER_PAIR * BLOCK, 2 * BLOCK), lambda p, i: (0, p, 0, 0))],
        out_specs=rows,
        out_shape=jax.ShapeDtypeStruct((b, seq, W_A), F32),
        scratch_shapes=scratch,
        compiler_params=_params(2),
        name="attn_prompt",
    )(q, k_t, v_t, bias_tabs)


SAMPLE_HEADS = 8
SAMPLE_LANES = SAMPLE_HEADS * HEAD_DIM


def _attn_sample_body(q_ref, kn_ref, vn_ref, ckt_ref, cvt_ref, bo_ref, bn_ref, o_ref, nkt_ref, nvt_ref,
                      *, wb, n_new):
    rows = SAMPLE_HEADS * n_new
    row_head = lax.broadcasted_iota(jnp.int32, (rows, SAMPLE_LANES), 0) // n_new
    col_head = lax.broadcasted_iota(jnp.int32, (rows, SAMPLE_LANES), 1) // HEAD_DIM
    own_head = row_head == col_head
    q_rep = jnp.concatenate([q_ref[...]] * SAMPLE_HEADS, axis=0)
    q_bd = jnp.where(own_head, q_rep, 0.0).astype(BF16)

    pad = jnp.zeros((LANES - n_new, SAMPLE_LANES), F32)
    kn = jnp.concatenate([kn_ref[...], pad], axis=0)
    vn = jnp.concatenate([vn_ref[...], pad], axis=0)
    kt = ckt_ref[...]
    vt = cvt_ref[...]

    s_old = jnp.dot(q_bd, kt.astype(BF16), preferred_element_type=F32)
    s_new = lax.dot_general(q_bd, kn.astype(BF16), NT_DIMS, preferred_element_type=F32)
    lo = [s_old + bo_ref[d] for d in range(N_DIL)]
    ln = [s_new + bn_ref[d] for d in range(N_DIL)]
    row_max = lambda t: jnp.max(t, axis=-1, keepdims=True)
    mx = functools.reduce(jnp.maximum, [row_max(t) for t in lo + ln])
    e_old = functools.reduce(lambda a, b: a + b, [jnp.exp(t - mx) for t in lo])
    e_new = functools.reduce(lambda a, b: a + b, [jnp.exp(t - mx) for t in ln])
    den = jnp.sum(e_old, axis=-1, keepdims=True) + jnp.sum(e_new, axis=-1, keepdims=True)
    o_full = lax.dot_general(e_old.astype(BF16), vt.astype(BF16), NT_DIMS, preferred_element_type=F32)
    o_full = o_full + jnp.dot(e_new.astype(BF16), vn.astype(BF16), preferred_element_type=F32)
    o_full = jnp.where(own_head, o_full / den, 0.0)
    out = o_full[0:n_new, :]
    for h in range(1, SAMPLE_HEADS):
        out = out + o_full[h * n_new:(h + 1) * n_new, :]
    o_ref[...] = out

    last = lax.broadcasted_iota(jnp.int32, (SAMPLE_LANES, LANES), 1) >= LANES - n_new
    for c_t, new, out_ref in ((kt, kn, nkt_ref), (vt, vn, nvt_ref)):
        rolled = pltpu.roll(c_t, wb - n_new, axis=1)
        out_ref[...] = rolled
        tail = pltpu.roll(new.T, LANES - n_new, axis=1)
        out_ref[:, wb - LANES:wb] = jnp.where(last, tail, rolled[:, wb - LANES:wb])


def _attn_sample(q, k_new, v_new, cache_kt, cache_vt, bias_old, bias_new):
    b, n_new, _ = q.shape
    wb = cache_kt.shape[2]
    n_groups = N_HEADS // SAMPLE_HEADS
    new_spec = pl.BlockSpec((None, n_new, SAMPLE_LANES), lambda i, g: (i, 0, g))
    cache_spec = pl.BlockSpec((None, SAMPLE_LANES, wb), lambda i, g: (i, g, 0))
    rows = SAMPLE_HEADS * n_new
    return pl.pallas_call(
        functools.partial(_attn_sample_body, wb=wb, n_new=n_new),
        grid=(b, n_groups),
        in_specs=[new_spec, new_spec, new_spec, cache_spec, cache_spec,
                  pl.BlockSpec((N_DIL, rows, wb), lambda i, g: (0, g, 0)),
                  pl.BlockSpec((N_DIL, rows, LANES), lambda i, g: (0, g, 0))],
        out_specs=[new_spec, cache_spec, cache_spec],
        out_shape=[jax.ShapeDtypeStruct((b, n_new, W_A), F32),
                   jax.ShapeDtypeStruct((b, W_A, wb), F32),
                   jax.ShapeDtypeStruct((b, W_A, wb), F32)],
        compiler_params=_params(2),
        name="attn_sample",
    )(q, k_new, v_new, cache_kt, cache_vt, bias_old, bias_new)


N_PROMPT_IN, N_SAMPLE_IN, N_SAMPLE_OUT = 4, 7, 3
VMEM_LIMIT_BOTH = 60 * 1024 * 1024


def _attn_both_body(*refs, seq, wb, n_new):
    n_in = N_PROMPT_IN + N_SAMPLE_IN
    prompt_in, sample_in = refs[:N_PROMPT_IN], refs[N_PROMPT_IN:n_in]
    o_ref, sample_out, scratch = refs[n_in], refs[n_in + 1:n_in + 1 + N_SAMPLE_OUT], refs[n_in + 1 + N_SAMPLE_OUT:]
    _attn_sample_body(*sample_in, *sample_out, wb=wb, n_new=n_new)
    _attn_prompt_body(*prompt_in, o_ref, *scratch, seq=seq)


def _attn_both(q, k_t, v_t, bias_prompt, qs, k_new, v_new, cache_kt, cache_vt, bias_old, bias_new):
    bp, seq, _ = q.shape
    bs, n_new, _ = qs.shape
    wb = cache_kt.shape[2]
    n_groups = N_HEADS // SAMPLE_HEADS
    assert N_PAIRS * bp == bs * n_groups, "one sample (batch, head group) per prompt (head pair, batch) step"
    step = lambda p, i: p * bp + i
    rows = pl.BlockSpec((None, seq, LANES), lambda p, i: (i, 0, p))
    cols = pl.BlockSpec((None, LANES, seq), lambda p, i: (i, p, 0))
    new_spec = pl.BlockSpec((None, n_new, SAMPLE_LANES), lambda p, i: (step(p, i) // n_groups, 0, step(p, i) % n_groups))
    cache_spec = pl.BlockSpec((None, SAMPLE_LANES, wb), lambda p, i: (step(p, i) // n_groups, step(p, i) % n_groups, 0))
    s_rows = SAMPLE_HEADS * n_new
    scratch = [pltpu.VMEM((seq + BLOCK, LANES), F32)] * 2 + [pltpu.VMEM((seq, LANES), F32)] * 9
    return pl.pallas_call(
        functools.partial(_attn_both_body, seq=seq, wb=wb, n_new=n_new),
        grid=(N_PAIRS, bp),
        in_specs=[rows, cols, cols,
                  pl.BlockSpec((N_DIL, None, HEADS_PER_PAIR * BLOCK, 2 * BLOCK), lambda p, i: (0, p, 0, 0)),
                  new_spec, new_spec, new_spec, cache_spec, cache_spec,
                  pl.BlockSpec((N_DIL, s_rows, wb), lambda p, i: (0, step(p, i) % n_groups, 0)),
                  pl.BlockSpec((N_DIL, s_rows, LANES), lambda p, i: (0, step(p, i) % n_groups, 0))],
        out_specs=[rows, new_spec, cache_spec, cache_spec],
        out_shape=[jax.ShapeDtypeStruct((bp, seq, W_A), F32),
                   jax.ShapeDtypeStruct((bs, n_new, W_A), F32),
                   jax.ShapeDtypeStruct((bs, W_A, wb), F32),
                   jax.ShapeDtypeStruct((bs, W_A, wb), F32)],
        scratch_shapes=scratch,
        compiler_params=_params(2, VMEM_LIMIT_BOTH),
        name="attn_both",
    )(q, k_t, v_t, bias_prompt, qs, k_new, v_new, cache_kt, cache_vt, bias_old, bias_new)


def _lru_core(xr_ref, gr_ref, cpast_ref, hpast_ref, cw_ref, cb_ref, wg_ref, bgx_ref, bga_ref, lp_ref, nl_ref,
              nconv_ref, nlru_ref, tail_ref, h_ref, hs_ref, tt):
    t = pl.program_id(1)
    n_t = pl.num_programs(1)

    @pl.when(t == 0)
    def _():
        tail_ref[...] = jnp.concatenate(
            [jnp.zeros((SUBLANES - (CONV_W - 1), W_R), F32), cpast_ref[...]], axis=0)
        h_ref[...] = hpast_ref[...]

    x = xr_ref[...]
    xe = jnp.concatenate([tail_ref[...], x], axis=0)
    xc = cb_ref[...] + cw_ref[CONV_W - 1:CONV_W, :] * x
    for tap in range(CONV_W - 1):
        back = CONV_W - 1 - tap
        xc = xc + cw_ref[tap:tap + 1, :] * xe[SUBLANES - back:SUBLANES - back + tt, :]
    tail_ref[...] = x[tt - SUBLANES:tt, :]

    xcb = xc.astype(BF16)
    gx, ga = [], []
    for j in range(W_R // MXU_TILE):
        g = jnp.dot(xcb[:, j * MXU_TILE:(j + 1) * MXU_TILE], wg_ref[j], preferred_element_type=F32)
        gx.append(g[:, :MXU_TILE])
        ga.append(g[:, MXU_TILE:])
    gate_x = jax.nn.sigmoid(jnp.concatenate(gx, axis=-1) + bgx_ref[...])
    gate_a = jax.nn.sigmoid(jnp.concatenate(ga, axis=-1) + bga_ref[...])
    log_a = (-LRU_C) * gate_a * jax.nn.softplus(-lp_ref[...])
    a = jnp.exp(log_a)
    var = jnp.tanh(-log_a) * (1.0 + a * a)
    root = jnp.where(var > 0.0, var * lax.rsqrt(var), 0.0)
    bx = root * (gate_x * xc)

    groups = tt // SUBLANES
    a3 = a.reshape(groups, SUBLANES, W_R)
    b3 = bx.reshape(groups, SUBLANES, W_R)
    sub = lax.broadcasted_iota(jnp.int32, (groups, SUBLANES, W_R), 1)
    shift = 1
    while shift < SUBLANES:
        keep = sub >= shift
        b_prev = jnp.where(keep, pltpu.roll(b3, shift, axis=1), 0.0)
        a_prev = jnp.where(keep, pltpu.roll(a3, shift, axis=1), 1.0)
        b3 = b3 + a3 * b_prev
        a3 = a3 * a_prev
        shift *= 2
    h = h_ref[...]
    for g in range(groups):
        hg = a3[g] * h + b3[g]
        hs_ref[g * SUBLANES:(g + 1) * SUBLANES, :] = hg
        h = hg[SUBLANES - 1:SUBLANES, :]
    h_ref[...] = h

    @pl.when(t == n_t - 1)
    def _():
        nconv_ref[...] = x[tt - (CONV_W - 1):tt, :]
        nlru_ref[...] = h

    o = hs_ref[...]
    return (((o * _rms_scale(o)) * nl_ref[...]) * jax.nn.silu(gr_ref[...])).astype(BF16)


N_LRU_IN = 11


def _lru_body(*refs, tt):
    ins, (y_ref, nconv_ref, nlru_ref), scratch = refs[:N_LRU_IN], refs[N_LRU_IN:N_LRU_IN + 3], refs[N_LRU_IN + 3:]
    y_ref[...] = _lru_core(*ins, nconv_ref, nlru_ref, *scratch, tt)


def _lru_specs(b, t_len, tt):
    assert t_len % tt == 0 and tt % SUBLANES == 0 and tt >= SUBLANES
    seq_spec = pl.BlockSpec((None, tt, W_R), lambda i, j: (i, j, 0))
    vec = pl.BlockSpec((1, W_R), lambda i, j: (0, 0))
    conv_spec = pl.BlockSpec((None, CONV_W - 1, W_R), lambda i, j: (i, 0, 0))
    h_spec = pl.BlockSpec((None, 1, W_R), lambda i, j: (i, 0, 0))
    in_specs = [seq_spec, seq_spec, conv_spec, h_spec,
                pl.BlockSpec((CONV_W, W_R), lambda i, j: (0, 0)), vec,
                pl.BlockSpec((W_R // MXU_TILE, MXU_TILE, 2 * MXU_TILE), lambda i, j: (0, 0, 0)),
                vec, vec, vec, vec]
    state_shapes = [jax.ShapeDtypeStruct((b, CONV_W - 1, W_R), F32), jax.ShapeDtypeStruct((b, 1, W_R), F32)]
    scratch = [pltpu.VMEM((SUBLANES, W_R), F32), pltpu.VMEM((1, W_R), F32), pltpu.VMEM((tt, W_R), F32)]
    return seq_spec, vec, in_specs, [conv_spec, h_spec], state_shapes, scratch


def _lru(x_r, g_r, conv_past, h_past, conv_w, conv_b, w_gates, b_gate_x, b_gate_a, lru_param, norm_lru, tt):
    b, t_len, _ = x_r.shape
    seq_spec, _, in_specs, state_specs, state_shapes, scratch = _lru_specs(b, t_len, tt)
    return pl.pallas_call(
        functools.partial(_lru_body, tt=tt),
        grid=(b, t_len // tt),
        in_specs=in_specs,
        out_specs=[seq_spec] + state_specs,
        out_shape=[jax.ShapeDtypeStruct((b, t_len, W_R), BF16)] + state_shapes,
        scratch_shapes=scratch,
        compiler_params=_params(2),
        name="rglru",
    )(x_r, g_r, conv_past, h_past, conv_w, conv_b, w_gates, b_gate_x, b_gate_a, lru_param, norm_lru)


def _gate_weight_tiles(w_gate_x, w_gate_a):
    per_tile = MXU_TILE // LRU_BLOCK
    eye = jnp.eye(per_tile, dtype=F32)

    def tiles(w):
        w = w.reshape(N_LRU_BLOCKS // per_tile, per_tile, LRU_BLOCK, LRU_BLOCK)
        bd = jnp.einsum('jaik,ab->jaibk', w, eye)
        return bd.reshape(N_LRU_BLOCKS // per_tile, MXU_TILE, MXU_TILE)
    return jnp.concatenate([tiles(w_gate_x), tiles(w_gate_a)], axis=-1).astype(BF16)


def _outproj_core(o_ref, ga_ref, y_lru, x_ref, na_ref, nf_ref, w_ref):
    o = o_ref[...]
    ya = ((o * _rms_scale(o)) * na_ref[...]) * jax.nn.silu(ga_ref[...])
    acc = jnp.dot(ya.astype(BF16), w_ref[0:W_A, :], preferred_element_type=F32)
    acc = acc + jnp.dot(y_lru, w_ref[W_A:W_A + W_R, :], preferred_element_type=F32)
    y = x_ref[...] + acc
    return (y * _rms_scale(y)) * nf_ref[...]


def _outproj_body(o_ref, ga_ref, yl_ref, x_ref, na_ref, nf_ref, w_ref, y_ref):
    y_ref[...] = _outproj_core(o_ref, ga_ref, yl_ref[...], x_ref, na_ref, nf_ref, w_ref)


N_OUT_IN = 6


def _lru_outproj_body(*refs, tt):
    lru_in, out_in = refs[:N_LRU_IN], refs[N_LRU_IN:N_LRU_IN + N_OUT_IN]
    y_ref, nconv_ref, nlru_ref = refs[N_LRU_IN + N_OUT_IN:N_LRU_IN + N_OUT_IN + 3]
    scratch = refs[N_LRU_IN + N_OUT_IN + 3:]
    o_ref, ga_ref, x_ref, na_ref, nf_ref, w_ref = out_in
    y_lru = _lru_core(*lru_in, nconv_ref, nlru_ref, *scratch, tt)
    y_ref[...] = _outproj_core(o_ref, ga_ref, y_lru, x_ref, na_ref, nf_ref, w_ref)


def _lru_outproj(x_r, g_r, conv_past, h_past, lru_params, o_att, g_a, x, norm_attn, norm_final, w_out_bf16, tt):
    b, t_len, _ = x_r.shape
    seq_spec, vec, in_specs, state_specs, state_shapes, scratch = _lru_specs(b, t_len, tt)
    return pl.pallas_call(
        functools.partial(_lru_outproj_body, tt=tt),
        grid=(b, t_len // tt),
        in_specs=in_specs + [seq_spec, seq_spec, seq_spec, vec, vec, _resident(w_out_bf16.shape)],
        out_specs=[seq_spec] + state_specs,
        out_shape=[jax.ShapeDtypeStruct((b, t_len, D_MODEL), F32)] + state_shapes,
        scratch_shapes=scratch,
        compiler_params=_params(2),
        name="rglru_outproj",
    )(x_r, g_r, conv_past, h_past, *lru_params, o_att, g_a, x, norm_attn, norm_final, w_out_bf16)


def _outproj(o_att, g_a, y_lru, x2d, norm_attn, norm_final, w_out_bf16, tm):
    m = x2d.shape[0]
    row = pl.BlockSpec((tm, D_MODEL), lambda i: (i, 0))
    vec = pl.BlockSpec((1, D_MODEL), lambda i: (0, 0))
    return pl.pallas_call(
        _outproj_body,
        grid=(m // tm,),
        in_specs=[row, row, row, row, vec, vec, pl.BlockSpec((W_A + W_R, D_MODEL), lambda i: (0, 0))],
        out_specs=row,
        out_shape=jax.ShapeDtypeStruct((m, D_MODEL), F32),
        compiler_params=_params(1),
        name="outproj",
    )(o_att, g_a, y_lru, x2d, norm_attn, norm_final, w_out_bf16)


PROMPT_ROWS = 512
LRU_TILE = 256


def _to_head_major(win):
    b, p = win.shape[0], win.shape[1]
    return jnp.transpose(win, (0, 2, 3, 1)).reshape(b, W_A, p)


def _from_head_major(win_t):
    b, _, p = win_t.shape
    return jnp.transpose(win_t.reshape(b, N_HEADS, HEAD_DIM, p), (0, 3, 1, 2))[None]


def kernel(x_prompt, x_sample, cache_win_k, cache_win_v, state_conv, state_lru, rel_bias, norm_in, w_in, norm_attn, norm_lru, conv_w, conv_b, w_gate_x, b_gate_x, w_gate_a, b_gate_a, lru_param, w_out, norm_final):
    depth = w_in.shape[0]
    assert depth == 1, "single-layer trunk"
    bp, seq, _ = x_prompt.shape
    bs, n_new, _ = x_sample.shape
    wb = cache_win_k.shape[2]
    row = lambda p: p.reshape(1, -1)

    w_in_b = w_in[0].astype(BF16)
    group = lambda i: w_in_b[:, i * W_A:(i + 1) * W_A]
    w_rows = jnp.concatenate([group(0), group(3), group(4), group(5)], axis=1)
    w_kv_t = jnp.concatenate([group(1).T, group(2).T], axis=0)
    w_out_b = w_out[0].astype(BF16)
    w_gates = _gate_weight_tiles(w_gate_x[0], w_gate_a[0])
    lru_args = (conv_w[0], row(conv_b[0]), w_gates, row(b_gate_x[0]), row(b_gate_a[0]), row(lru_param[0]), row(norm_lru[0]))
    bias_prompt, bias_old, bias_new = _bias_tables(rel_bias, n_new, wb)

    q, g_a, x_r, g_r, k_t, v_t = _inproj_prompt(x_prompt, row(norm_in[0]), w_rows, w_kv_t, PROMPT_ROWS)
    xs = x_sample.reshape(bs * n_new, D_MODEL)
    qs, ks, vs, gas, xrs, grs = _inproj_rows(xs, row(norm_in[0]), w_in_b)
    as_new = lambda a: a.reshape(bs, n_new, -1)

    o_att, o_s, new_kt, new_vt = _attn_both(
        q, k_t, v_t, bias_prompt.reshape(N_DIL, N_PAIRS, HEADS_PER_PAIR * BLOCK, 2 * BLOCK),
        as_new(qs), as_new(ks), as_new(vs), _to_head_major(cache_win_k[0]), _to_head_major(cache_win_v[0]),
        bias_old.reshape(N_DIL, N_HEADS * n_new, wb), bias_new.reshape(N_DIL, N_HEADS * n_new, LANES))

    y_prompt, conv_p, lru_p = _lru_outproj(
        x_r, g_r, jnp.zeros((bp, CONV_W - 1, W_R), F32), jnp.zeros((bp, 1, W_R), state_lru.dtype), lru_args,
        o_att, g_a, x_prompt, row(norm_attn[0]), row(norm_final), w_out_b, tt=LRU_TILE)
    y_lru_s, conv_s, lru_s = _lru(as_new(xrs), as_new(grs), state_conv[0], state_lru[0].reshape(bs, 1, W_R),
                                  *lru_args, tt=n_new)
    y_sample = _outproj(o_s.reshape(bs * n_new, W_A), gas, y_lru_s.reshape(bs * n_new, W_R), xs,
                        row(norm_attn[0]), row(norm_final), w_out_b, bs * n_new)

    return (y_prompt.reshape(bp, seq, D_MODEL), y_sample.reshape(bs, n_new, D_MODEL),
            _from_head_major(k_t), _from_head_major(v_t), conv_p[None], lru_p.reshape(1, bp, W_R),
            _from_head_major(new_kt), _from_head_major(new_vt), conv_s[None],
            lru_s.reshape(1, bs, W_R).astype(state_lru.dtype))
```

```python
import functools
import math

import numpy as np
import jax
import jax.numpy as jnp
from jax import lax
from jax.experimental import pallas as pl
from jax.experimental.pallas import tpu as pltpu

D_MODEL = 1024
HEAD_DIM = 64
N_HEADS = 16
W_A = N_HEADS * HEAD_DIM
W_R = D_MODEL
N_LRU_BLOCKS = 16
LRU_BLOCK = W_R // N_LRU_BLOCKS
N_PROJ = 6
CONV_W = 4
LRU_C = 8.0
DILATIONS = (1, 4, 16)
N_DIL = len(DILATIONS)
N_STEPS = 128
BLOCK = 128
N_BUCKETS = 32
MAX_EXACT = 16
MAX_DISTANCE = 2048
EPS = 1e-6
SCALE = HEAD_DIM ** -0.5
NEG_INF = -1e30

LANES = 128
SUBLANES = 8
HEADS_PER_PAIR = LANES // HEAD_DIM
N_PAIRS = N_HEADS // HEADS_PER_PAIR
MXU_TILE = 256
VMEM_LIMIT = 56 * 1024 * 1024

F32 = jnp.float32
BF16 = jnp.bfloat16
NT_DIMS = (((1,), (1,)), ((), ()))


def _params(n_axes, vmem_limit=VMEM_LIMIT):
    return pltpu.CompilerParams(dimension_semantics=("arbitrary",) * n_axes,
                                vmem_limit_bytes=vmem_limit)


def _rms_scale(x):
    return lax.rsqrt(jnp.mean(x * x, axis=-1, keepdims=True) + EPS)


def _resident(shape):
    return pl.BlockSpec(shape, lambda *_: (0,) * len(shape), pipeline_mode=pl.Buffered(1))


def _normed(x_ref, g_ref):
    x = x_ref[...]
    return ((x * _rms_scale(x)) * g_ref[...]).astype(BF16)


def _inproj_rows_body(x_ref, g_ref, w_ref, *out_refs):
    h = _normed(x_ref, g_ref)
    for i, o_ref in enumerate(out_refs):
        r = jnp.dot(h, w_ref[:, i * W_A:(i + 1) * W_A], preferred_element_type=F32)
        if i == 0:
            r = r * SCALE
        o_ref[...] = r


def _inproj_rows(x2d, norm_in, w_in_bf16):
    m = x2d.shape[0]
    row = pl.BlockSpec((m, D_MODEL), lambda i: (0, 0))
    return pl.pallas_call(
        _inproj_rows_body,
        grid=(1,),
        in_specs=[row, pl.BlockSpec((1, D_MODEL), lambda i: (0, 0)),
                  pl.BlockSpec((D_MODEL, N_PROJ * W_A), lambda i: (0, 0))],
        out_specs=[row] * N_PROJ,
        out_shape=[jax.ShapeDtypeStruct((m, W_A), F32)] * N_PROJ,
        compiler_params=_params(1),
        name="inproj_rows",
    )(x2d, norm_in, w_in_bf16)


def _inproj_prompt_body(x_ref, g_ref, wr_ref, wt_ref, q_ref, ga_ref, xr_ref, gr_ref, kt_ref, vt_ref):
    h = _normed(x_ref, g_ref)
    for i, o_ref in enumerate((q_ref, ga_ref, xr_ref, gr_ref)):
        r = jnp.dot(h, wr_ref[:, i * W_A:(i + 1) * W_A], preferred_element_type=F32)
        if i == 0:
            r = r * SCALE
        o_ref[...] = r
    for i, o_ref in enumerate((kt_ref, vt_ref)):
        o_ref[...] = lax.dot_general(wt_ref[i * W_A:(i + 1) * W_A, :], h, NT_DIMS, preferred_element_type=F32)


def _inproj_prompt(x, norm_in, w_rows, w_kv_t, tm):
    b, seq, _ = x.shape
    row = pl.BlockSpec((None, tm, D_MODEL), lambda i, j: (i, j, 0))
    col = pl.BlockSpec((None, W_A, tm), lambda i, j: (i, 0, j))
    rows_shape = jax.ShapeDtypeStruct((b, seq, W_A), F32)
    cols_shape = jax.ShapeDtypeStruct((b, W_A, seq), F32)
    return pl.pallas_call(
        _inproj_prompt_body,
        grid=(b, seq // tm),
        in_specs=[row, _resident((1, D_MODEL)), _resident(w_rows.shape), _resident(w_kv_t.shape)],
        out_specs=[row] * 4 + [col] * 2,
        out_shape=[rows_shape] * 4 + [cols_shape] * 2,
        compiler_params=_params(2),
        name="inproj_prompt",
    )(x, norm_in, w_rows, w_kv_t)


def _t5_bucket_np(dist):
    nf = np.maximum(dist, 1).astype(np.float32)
    large = MAX_EXACT + (np.log(nf / np.float32(MAX_EXACT)) / np.float32(math.log(MAX_DISTANCE / MAX_EXACT))
                         * np.float32(N_BUCKETS - MAX_EXACT)).astype(np.int32)
    large = np.minimum(large, N_BUCKETS - 1)
    return np.where(dist < MAX_EXACT, dist, large).astype(np.int32)


def _bucket_index_tables(n_new, wb):
    i = np.arange(BLOCK)[:, None]
    j = np.arange(2 * BLOCK)[None, :]
    diff = i - j + BLOCK
    band = (diff >= 0) & (diff <= N_STEPS)
    t = np.arange(n_new)[:, None]
    dist_old = wb + t - np.arange(wb)[None, :]
    dist_new = t - np.arange(LANES)[None, :]
    prompt, s_old, s_new = [], [], []
    for dil in DILATIONS:
        prompt.append(np.where(band, _t5_bucket_np(np.maximum(diff, 0) * dil), -1))
        for dist, out, extra in ((dist_old, s_old, True), (dist_new, s_new, np.arange(LANES)[None, :] < n_new)):
            ok = (dist >= 0) & (dist % dil == 0) & (dist // dil <= N_STEPS) & extra
            out.append(np.where(ok, _t5_bucket_np(np.maximum(dist, 0)), -1))
    as_i32 = lambda a: np.stack(a).astype(np.int32)
    return as_i32(prompt), as_i32(s_old), as_i32(s_new)


def _bias_tables_body(rb_ref, *refs, buckets):
    idx_refs, out_refs = refs[:3], refs[3:]
    for d in range(N_DIL):
        idx = [r[d] for r in idx_refs]

        def per_head(h, carry):
            tabs = [jnp.full(ix.shape, NEG_INF, F32) for ix in idx]
            for b in buckets[d]:
                val = rb_ref[b, h]
                tabs = [jnp.where(ix == b, val, tb) for ix, tb in zip(idx, tabs)]
            for o_ref, tb in zip(out_refs, tabs):
                o_ref[d, h] = tb
            return carry
        lax.fori_loop(0, N_HEADS, per_head, 0)


def _bias_tables(rel_bias, n_new, wb):
    idx = _bucket_index_tables(n_new, wb)
    buckets = tuple(tuple(int(b) for b in np.unique(np.concatenate([a[d].ravel() for a in idx])) if b >= 0)
                    for d in range(N_DIL))
    vmem = pl.BlockSpec(memory_space=pltpu.VMEM)
    return pl.pallas_call(
        functools.partial(_bias_tables_body, buckets=buckets),
        in_specs=[pl.BlockSpec(memory_space=pltpu.SMEM)] + [vmem] * 3,
        out_specs=[vmem] * 3,
        out_shape=[jax.ShapeDtypeStruct((N_DIL, N_HEADS) + a.shape[1:], F32) for a in idx],
        compiler_params=pltpu.CompilerParams(vmem_limit_bytes=VMEM_LIMIT),
        name="bias_tables",
    )(rel_bias, *[jnp.asarray(a) for a in idx])


INTERLEAVE = 4


def _prompt_scratch(seq):
    n_blocks = seq // BLOCK
    return ([pltpu.VMEM((seq + BLOCK, LANES), F32)] * 2 + [pltpu.VMEM((seq, LANES), F32)] * 9
            + [pltpu.VMEM((n_blocks, HEADS_PER_PAIR * BLOCK, 2 * BLOCK), BF16)])


def _attn_prompt_body(q_ref, kt_ref, vt_ref, bm_ref, o_ref, k_ref, v_ref, qg_ref, kg_ref, vg_ref,
                      acc16_ref, m16_ref, l16_ref, accp_ref, mp_ref, lp_ref, e_ref, *, seq, extra_work=None):
    group = seq // INTERLEAVE
    nb4 = group // BLOCK
    n_blocks = seq // BLOCK
    assert DILATIONS == (1, INTERLEAVE, INTERLEAVE * INTERLEAVE) and seq == DILATIONS[2] * BLOCK

    first = lax.broadcasted_iota(jnp.int32, (BLOCK, LANES), 1) < HEAD_DIM
    pick = lambda a: jnp.where(first, a[:BLOCK], a[BLOCK:])
    grouped = (qg_ref, kg_ref, vg_ref)
    natural = (q_ref, k_ref, v_ref)

    one_trip = jnp.minimum(pl.program_id(0) + 1, 1)

    def region(fn):
        def body(_, carry):
            fn()
            return carry
        lax.fori_loop(0, one_trip, body, 0)

    def rows_of(start, n, stride):
        return pl.ds(start, n) if stride == 1 else pl.ds(start, n, stride=stride)

    def scores(slot, dil_idx, refs, stride, q_start, k_start, with_prev, mask_prev=False):
        qs_ref, ks_ref, _ = refs
        q = qs_ref[rows_of(q_start, BLOCK, stride), :]
        q2 = jnp.concatenate([jnp.where(first, q, 0.0), jnp.where(first, 0.0, q)], axis=0).astype(BF16)
        nk = 2 * BLOCK if with_prev else BLOCK
        kb = ks_ref[rows_of(k_start, nk, stride), :].astype(BF16)
        bias = bm_ref[dil_idx] if with_prev else bm_ref[dil_idx, :, BLOCK:]
        s = lax.dot_general(q2, kb, NT_DIMS, preferred_element_type=F32) + bias
        if mask_prev:
            prev_cols = lax.broadcasted_iota(jnp.int32, s.shape, 1) < BLOCK
            s = jnp.where(prev_cols, NEG_INF, s)
        m = jnp.max(s, axis=-1, keepdims=True)
        e_ref[slot, :, 0:nk] = jnp.exp(s - m).astype(BF16)
        return pick(m)

    def values(slot, refs, stride, k_start, with_prev):
        nk = 2 * BLOCK if with_prev else BLOCK
        vb = refs[2][rows_of(k_start, nk, stride), :].astype(BF16)
        v1 = jnp.concatenate([vb, jnp.ones((nk, LANES), BF16)], axis=1)
        acc = jnp.dot(e_ref[slot, :, 0:nk], v1, preferred_element_type=F32)
        return pick(acc[:, :LANES]), pick(acc[:, LANES:])

    def merge(parts):
        mx = functools.reduce(jnp.maximum, [m for _, m, _ in parts])
        scale = [jnp.exp(m - mx) for _, m, _ in parts]
        acc = functools.reduce(lambda x, y: x + y, [w * a for w, (a, _, _) in zip(scale, parts)])
        den = functools.reduce(lambda x, y: x + y, [w * l for w, (_, _, l) in zip(scale, parts)])
        return acc, mx, den

    d16_start = [(idx // INTERLEAVE) * group + idx % INTERLEAVE for idx in range(n_blocks)]
    d16_rows = [pl.ds(st, BLOCK, stride=INTERLEAVE) for st in d16_start]
    d4_blocks = [(g * group + n * BLOCK, n > 0) for g in range(INTERLEAVE) for n in range(nb4)]
    d4_rows = [pl.ds(g + n * BLOCK * INTERLEAVE, BLOCK, stride=INTERLEAVE) for g in range(INTERLEAVE) for n in range(nb4)]

    def setup_and_d16_scores():
        for t_ref, r_ref in ((kt_ref, k_ref), (vt_ref, v_ref)):
            r_ref[0:BLOCK, :] = jnp.zeros((BLOCK, LANES), F32)
            r_ref[BLOCK:BLOCK + seq, :] = t_ref[...].T
        for src, dst, off in ((q_ref, qg_ref, 0), (k_ref, kg_ref, BLOCK), (v_ref, vg_ref, BLOCK)):
            for g in range(INTERLEAVE):
                dst[g * group:(g + 1) * group, :] = src[pl.ds(off + g, group, stride=INTERLEAVE), :]
        for slot, st in enumerate(d16_start):
            m16_ref[d16_rows[slot], :] = scores(slot, 2, grouped, INTERLEAVE, st, st, False)

    def d16_values_and_d4_scores():
        for slot, st in enumerate(d16_start):
            acc, l = values(slot, grouped, INTERLEAVE, st, False)
            acc16_ref[d16_rows[slot], :] = acc
            l16_ref[d16_rows[slot], :] = l
        for slot, (st, with_prev) in enumerate(d4_blocks):
            accp_ref[d4_rows[slot], :] = scores(slot, 1, grouped, 1, st, st - BLOCK if with_prev else st, with_prev)

    def d4_values_and_d1_scores():
        for slot, (st, with_prev) in enumerate(d4_blocks):
            acc, l = values(slot, grouped, 1, st - BLOCK if with_prev else st, with_prev)
            rows = pl.ds(st, BLOCK)
            vals = merge([(acc, accp_ref[d4_rows[slot], :], l),
                          (acc16_ref[rows, :], m16_ref[rows, :], l16_ref[rows, :])])
            for ref, val in zip((accp_ref, mp_ref, lp_ref), vals):
                ref[d4_rows[slot], :] = val
        for n in range(n_blocks):
            o_ref[pl.ds(n * BLOCK, BLOCK), :] = scores(n, 0, natural, 1, n * BLOCK, n * BLOCK, True, mask_prev=n == 0)

    def d1_values():
        for n in range(n_blocks):
            rows = pl.ds(n * BLOCK, BLOCK)
            acc, l = values(n, natural, 1, n * BLOCK, True)
            acc, _, den = merge([(acc, o_ref[rows, :], l), (accp_ref[rows, :], mp_ref[rows, :], lp_ref[rows, :])])
            o_ref[rows, :] = acc / den

    stages = (setup_and_d16_scores, d16_values_and_d4_scores, d4_values_and_d1_scores, d1_values)
    for stage, extra in zip(stages, extra_work or (None,) * len(stages)):
        region(stage if extra is None else lambda stage=stage, extra=extra: (stage(), extra()))


def _attn_prompt(q, k_t, v_t, bias_tabs):
    b, seq, _ = q.shape
    rows = pl.BlockSpec((None, seq, LANES), lambda p, i: (i, 0, p))
    cols = pl.BlockSpec((None, LANES, seq), lambda p, i: (i, p, 0))
    scratch = _prompt_scratch(seq)
    return pl.pallas_call(
        functools.partial(_attn_prompt_body, seq=seq),
        grid=(N_PAIRS, b),
        in_specs=[rows, cols, cols,
                  pl.BlockSpec((N_DIL, None, HEADS_PER_PAIR * BLOCK, 2 * BLOCK), lambda p, i: (0, p, 0, 0))],
        out_specs=rows,
        out_shape=jax.ShapeDtypeStruct((b, seq, W_A), F32),
        scratch_shapes=scratch,
        compiler_params=_params(2),
        name="attn_prompt",
    )(q, k_t, v_t, bias_tabs)


SAMPLE_HEADS = 8
SAMPLE_LANES = SAMPLE_HEADS * HEAD_DIM


def _padded_new(new_ref, n_new):
    return jnp.concatenate([new_ref[...], jnp.zeros((LANES - n_new, SAMPLE_LANES), F32)], axis=0)


def _window_update(c_ref, new_ref, out_ref, *, wb, n_new):
    last = lax.broadcasted_iota(jnp.int32, (SAMPLE_LANES, LANES), 1) >= LANES - n_new
    rolled = pltpu.roll(c_ref[...], wb - n_new, axis=1)
    out_ref[...] = rolled
    tail = pltpu.roll(_padded_new(new_ref, n_new).T, LANES - n_new, axis=1)
    out_ref[:, wb - LANES:wb] = jnp.where(last, tail, rolled[:, wb - LANES:wb])


def _attn_sample_body(q_ref, kn_ref, vn_ref, ckt_ref, cvt_ref, bo_ref, bn_ref, o_ref, nkt_ref, nvt_ref,
                      *, wb, n_new):
    _sample_attention(q_ref, kn_ref, vn_ref, ckt_ref, cvt_ref, bo_ref, bn_ref, o_ref, n_new=n_new)
    _window_update(ckt_ref, kn_ref, nkt_ref, wb=wb, n_new=n_new)
    _window_update(cvt_ref, vn_ref, nvt_ref, wb=wb, n_new=n_new)


def _sample_attention(q_ref, kn_ref, vn_ref, ckt_ref, cvt_ref, bo_ref, bn_ref, o_ref, *, n_new):
    rows = SAMPLE_HEADS * n_new
    row_head = lax.broadcasted_iota(jnp.int32, (rows, SAMPLE_LANES), 0) // n_new
    col_head = lax.broadcasted_iota(jnp.int32, (rows, SAMPLE_LANES), 1) // HEAD_DIM
    own_head = row_head == col_head
    q_rep = jnp.concatenate([q_ref[...]] * SAMPLE_HEADS, axis=0)
    q_bd = jnp.where(own_head, q_rep, 0.0).astype(BF16)

    kn = _padded_new(kn_ref, n_new)
    vn = _padded_new(vn_ref, n_new)
    kt = ckt_ref[...]
    vt = cvt_ref[...]

    s_old = jnp.dot(q_bd, kt.astype(BF16), preferred_element_type=F32)
    s_new = lax.dot_general(q_bd, kn.astype(BF16), NT_DIMS, preferred_element_type=F32)
    lo = [s_old + bo_ref[d] for d in range(N_DIL)]
    ln = [s_new + bn_ref[d] for d in range(N_DIL)]
    row_max = lambda t: jnp.max(t, axis=-1, keepdims=True)
    mx = functools.reduce(jnp.maximum, [row_max(t) for t in lo + ln])
    e_old = functools.reduce(lambda a, b: a + b, [jnp.exp(t - mx) for t in lo])
    e_new = functools.reduce(lambda a, b: a + b, [jnp.exp(t - mx) for t in ln])
    den = jnp.sum(e_old, axis=-1, keepdims=True) + jnp.sum(e_new, axis=-1, keepdims=True)
    o_full = lax.dot_general(e_old.astype(BF16), vt.astype(BF16), NT_DIMS, preferred_element_type=F32)
    o_full = o_full + jnp.dot(e_new.astype(BF16), vn.astype(BF16), preferred_element_type=F32)
    o_full = jnp.where(own_head, o_full / den, 0.0)
    out = o_full[0:n_new, :]
    for h in range(1, SAMPLE_HEADS):
        out = out + o_full[h * n_new:(h + 1) * n_new, :]
    o_ref[...] = out


def _attn_sample(q, k_new, v_new, cache_kt, cache_vt, bias_old, bias_new):
    b, n_new, _ = q.shape
    wb = cache_kt.shape[2]
    n_groups = N_HEADS // SAMPLE_HEADS
    new_spec = pl.BlockSpec((None, n_new, SAMPLE_LANES), lambda i, g: (i, 0, g))
    cache_spec = pl.BlockSpec((None, SAMPLE_LANES, wb), lambda i, g: (i, g, 0))
    rows = SAMPLE_HEADS * n_new
    return pl.pallas_call(
        functools.partial(_attn_sample_body, wb=wb, n_new=n_new),
        grid=(b, n_groups),
        in_specs=[new_spec, new_spec, new_spec, cache_spec, cache_spec,
                  pl.BlockSpec((N_DIL, rows, wb), lambda i, g: (0, g, 0)),
                  pl.BlockSpec((N_DIL, rows, LANES), lambda i, g: (0, g, 0))],
        out_specs=[new_spec, cache_spec, cache_spec],
        out_shape=[jax.ShapeDtypeStruct((b, n_new, W_A), F32),
                   jax.ShapeDtypeStruct((b, W_A, wb), F32),
                   jax.ShapeDtypeStruct((b, W_A, wb), F32)],
        compiler_params=_params(2),
        name="attn_sample",
    )(q, k_new, v_new, cache_kt, cache_vt, bias_old, bias_new)


N_PROMPT_IN, N_SAMPLE_IN, N_SAMPLE_OUT = 4, 7, 3
VMEM_LIMIT_BOTH = 62 * 1024 * 1024


def _attn_both_body(*refs, seq, wb, n_new):
    n_in = N_PROMPT_IN + N_SAMPLE_IN
    prompt_in, sample_in = refs[:N_PROMPT_IN], refs[N_PROMPT_IN:n_in]
    o_ref, sample_out, scratch = refs[n_in], refs[n_in + 1:n_in + 1 + N_SAMPLE_OUT], refs[n_in + 1 + N_SAMPLE_OUT:]
    extra_work = (None, None, None, functools.partial(_attn_sample_body, *sample_in, *sample_out, wb=wb, n_new=n_new))
    _attn_prompt_body(*prompt_in, o_ref, *scratch, seq=seq, extra_work=extra_work)


def _attn_both(q, k_t, v_t, bias_prompt, qs, k_new, v_new, cache_kt, cache_vt, bias_old, bias_new):
    bp, seq, _ = q.shape
    bs, n_new, _ = qs.shape
    wb = cache_kt.shape[2]
    n_groups = N_HEADS // SAMPLE_HEADS
    assert N_PAIRS * bp == bs * n_groups, "one sample (batch, head group) per prompt (head pair, batch) step"
    step = lambda p, i: p * bp + i
    rows = pl.BlockSpec((None, seq, LANES), lambda p, i: (i, 0, p))
    cols = pl.BlockSpec((None, LANES, seq), lambda p, i: (i, p, 0))
    new_spec = pl.BlockSpec((None, n_new, SAMPLE_LANES), lambda p, i: (step(p, i) // n_groups, 0, step(p, i) % n_groups))
    cache_spec = pl.BlockSpec((None, SAMPLE_LANES, wb), lambda p, i: (step(p, i) // n_groups, step(p, i) % n_groups, 0))
    s_rows = SAMPLE_HEADS * n_new
    scratch = _prompt_scratch(seq)
    return pl.pallas_call(
        functools.partial(_attn_both_body, seq=seq, wb=wb, n_new=n_new),
        grid=(N_PAIRS, bp),
        in_specs=[rows, cols, cols,
                  pl.BlockSpec((N_DIL, None, HEADS_PER_PAIR * BLOCK, 2 * BLOCK), lambda p, i: (0, p, 0, 0)),
                  new_spec, new_spec, new_spec, cache_spec, cache_spec,
                  pl.BlockSpec((N_DIL, s_rows, wb), lambda p, i: (0, step(p, i) % n_groups, 0)),
                  pl.BlockSpec((N_DIL, s_rows, LANES), lambda p, i: (0, step(p, i) % n_groups, 0))],
        out_specs=[rows, new_spec, cache_spec, cache_spec],
        out_shape=[jax.ShapeDtypeStruct((bp, seq, W_A), F32),
                   jax.ShapeDtypeStruct((bs, n_new, W_A), F32),
                   jax.ShapeDtypeStruct((bs, W_A, wb), F32),
                   jax.ShapeDtypeStruct((bs, W_A, wb), F32)],
        scratch_shapes=scratch,
        compiler_params=_params(2, VMEM_LIMIT_BOTH),
        name="attn_both",
    )(q, k_t, v_t, bias_prompt, qs, k_new, v_new, cache_kt, cache_vt, bias_old, bias_new)


def _lru_core(xr_ref, gr_ref, cpast_ref, hpast_ref, cw_ref, cb_ref, wg_ref, bgx_ref, bga_ref, lp_ref, nl_ref,
              nconv_ref, nlru_ref, tail_ref, h_ref, hs_ref, tt):
    t = pl.program_id(1)
    n_t = pl.num_programs(1)

    @pl.when(t == 0)
    def _():
        tail_ref[...] = jnp.concatenate(
            [jnp.zeros((SUBLANES - (CONV_W - 1), W_R), F32), cpast_ref[...]], axis=0)
        h_ref[...] = hpast_ref[...]

    x = xr_ref[...]
    xe = jnp.concatenate([tail_ref[...], x], axis=0)
    xc = cb_ref[...] + cw_ref[CONV_W - 1:CONV_W, :] * x
    for tap in range(CONV_W - 1):
        back = CONV_W - 1 - tap
        xc = xc + cw_ref[tap:tap + 1, :] * xe[SUBLANES - back:SUBLANES - back + tt, :]
    tail_ref[...] = x[tt - SUBLANES:tt, :]

    xcb = xc.astype(BF16)
    gx, ga = [], []
    for j in range(W_R // MXU_TILE):
        g = jnp.dot(xcb[:, j * MXU_TILE:(j + 1) * MXU_TILE], wg_ref[j], preferred_element_type=F32)
        gx.append(g[:, :MXU_TILE])
        ga.append(g[:, MXU_TILE:])
    gate_x = jax.nn.sigmoid(jnp.concatenate(gx, axis=-1) + bgx_ref[...])
    gate_a = jax.nn.sigmoid(jnp.concatenate(ga, axis=-1) + bga_ref[...])
    log_a = (-LRU_C) * gate_a * jax.nn.softplus(-lp_ref[...])
    a = jnp.exp(log_a)
    var = jnp.tanh(-log_a) * (1.0 + a * a)
    root = jnp.where(var > 0.0, var * lax.rsqrt(var), 0.0)
    bx = root * (gate_x * xc)

    groups = tt // SUBLANES
    a3 = a.reshape(groups, SUBLANES, W_R)
    b3 = bx.reshape(groups, SUBLANES, W_R)
    sub = lax.broadcasted_iota(jnp.int32, (groups, SUBLANES, W_R), 1)
    shift = 1
    while shift < SUBLANES:
        keep = sub >= shift
        b_prev = jnp.where(keep, pltpu.roll(b3, shift, axis=1), 0.0)
        a_prev = jnp.where(keep, pltpu.roll(a3, shift, axis=1), 1.0)
        b3 = b3 + a3 * b_prev
        a3 = a3 * a_prev
        shift *= 2
    h = h_ref[...]
    for g in range(groups):
        hg = a3[g] * h + b3[g]
        hs_ref[g * SUBLANES:(g + 1) * SUBLANES, :] = hg
        h = hg[SUBLANES - 1:SUBLANES, :]
    h_ref[...] = h

    @pl.when(t == n_t - 1)
    def _():
        nconv_ref[...] = x[tt - (CONV_W - 1):tt, :]
        nlru_ref[...] = h

    o = hs_ref[...]
    return (((o * _rms_scale(o)) * nl_ref[...]) * jax.nn.silu(gr_ref[...])).astype(BF16)


N_LRU_IN = 11


def _lru_body(*refs, tt):
    ins, (y_ref, nconv_ref, nlru_ref), scratch = refs[:N_LRU_IN], refs[N_LRU_IN:N_LRU_IN + 3], refs[N_LRU_IN + 3:]
    y_ref[...] = _lru_core(*ins, nconv_ref, nlru_ref, *scratch, tt)


def _lru_specs(b, t_len, tt):
    assert t_len % tt == 0 and tt % SUBLANES == 0 and tt >= SUBLANES
    seq_spec = pl.BlockSpec((None, tt, W_R), lambda i, j: (i, j, 0))
    vec = pl.BlockSpec((1, W_R), lambda i, j: (0, 0))
    conv_spec = pl.BlockSpec((None, CONV_W - 1, W_R), lambda i, j: (i, 0, 0))
    h_spec = pl.BlockSpec((None, 1, W_R), lambda i, j: (i, 0, 0))
    in_specs = [seq_spec, seq_spec, conv_spec, h_spec,
                pl.BlockSpec((CONV_W, W_R), lambda i, j: (0, 0)), vec,
                pl.BlockSpec((W_R // MXU_TILE, MXU_TILE, 2 * MXU_TILE), lambda i, j: (0, 0, 0)),
                vec, vec, vec, vec]
    state_shapes = [jax.ShapeDtypeStruct((b, CONV_W - 1, W_R), F32), jax.ShapeDtypeStruct((b, 1, W_R), F32)]
    scratch = [pltpu.VMEM((SUBLANES, W_R), F32), pltpu.VMEM((1, W_R), F32), pltpu.VMEM((tt, W_R), F32)]
    return seq_spec, vec, in_specs, [conv_spec, h_spec], state_shapes, scratch


def _lru(x_r, g_r, conv_past, h_past, conv_w, conv_b, w_gates, b_gate_x, b_gate_a, lru_param, norm_lru, tt):
    b, t_len, _ = x_r.shape
    seq_spec, _, in_specs, state_specs, state_shapes, scratch = _lru_specs(b, t_len, tt)
    return pl.pallas_call(
        functools.partial(_lru_body, tt=tt),
        grid=(b, t_len // tt),
        in_specs=in_specs,
        out_specs=[seq_spec] + state_specs,
        out_shape=[jax.ShapeDtypeStruct((b, t_len, W_R), BF16)] + state_shapes,
        scratch_shapes=scratch,
        compiler_params=_params(2),
        name="rglru",
    )(x_r, g_r, conv_past, h_past, conv_w, conv_b, w_gates, b_gate_x, b_gate_a, lru_param, norm_lru)


def _gate_weight_tiles(w_gate_x, w_gate_a):
    per_tile = MXU_TILE // LRU_BLOCK
    eye = jnp.eye(per_tile, dtype=F32)

    def tiles(w):
        w = w.reshape(N_LRU_BLOCKS // per_tile, per_tile, LRU_BLOCK, LRU_BLOCK)
        bd = jnp.einsum('jaik,ab->jaibk', w, eye)
        return bd.reshape(N_LRU_BLOCKS // per_tile, MXU_TILE, MXU_TILE)
    return jnp.concatenate([tiles(w_gate_x), tiles(w_gate_a)], axis=-1).astype(BF16)


def _outproj_core(o_ref, ga_ref, y_lru, x_ref, na_ref, nf_ref, w_ref):
    o = o_ref[...]
    ya = ((o * _rms_scale(o)) * na_ref[...]) * jax.nn.silu(ga_ref[...])
    acc = jnp.dot(ya.astype(BF16), w_ref[0:W_A, :], preferred_element_type=F32)
    acc = acc + jnp.dot(y_lru, w_ref[W_A:W_A + W_R, :], preferred_element_type=F32)
    y = x_ref[...] + acc
    return (y * _rms_scale(y)) * nf_ref[...]


def _outproj_body(o_ref, ga_ref, yl_ref, x_ref, na_ref, nf_ref, w_ref, y_ref):
    y_ref[...] = _outproj_core(o_ref, ga_ref, yl_ref[...], x_ref, na_ref, nf_ref, w_ref)


N_OUT_IN = 6


def _lru_outproj_body(*refs, tt):
    lru_in, out_in = refs[:N_LRU_IN], refs[N_LRU_IN:N_LRU_IN + N_OUT_IN]
    y_ref, nconv_ref, nlru_ref = refs[N_LRU_IN + N_OUT_IN:N_LRU_IN + N_OUT_IN + 3]
    scratch = refs[N_LRU_IN + N_OUT_IN + 3:]
    o_ref, ga_ref, x_ref, na_ref, nf_ref, w_ref = out_in
    y_lru = _lru_core(*lru_in, nconv_ref, nlru_ref, *scratch, tt)
    y_ref[...] = _outproj_core(o_ref, ga_ref, y_lru, x_ref, na_ref, nf_ref, w_ref)


def _lru_outproj(x_r, g_r, conv_past, h_past, lru_params, o_att, g_a, x, norm_attn, norm_final, w_out_bf16, tt):
    b, t_len, _ = x_r.shape
    seq_spec, vec, in_specs, state_specs, state_shapes, scratch = _lru_specs(b, t_len, tt)
    return pl.pallas_call(
        functools.partial(_lru_outproj_body, tt=tt),
        grid=(b, t_len // tt),
        in_specs=in_specs + [seq_spec, seq_spec, seq_spec, vec, vec, _resident(w_out_bf16.shape)],
        out_specs=[seq_spec] + state_specs,
        out_shape=[jax.ShapeDtypeStruct((b, t_len, D_MODEL), F32)] + state_shapes,
        scratch_shapes=scratch,
        compiler_params=_params(2),
        name="rglru_outproj",
    )(x_r, g_r, conv_past, h_past, *lru_params, o_att, g_a, x, norm_attn, norm_final, w_out_bf16)


def _outproj(o_att, g_a, y_lru, x2d, norm_attn, norm_final, w_out_bf16, tm):
    m = x2d.shape[0]
    row = pl.BlockSpec((tm, D_MODEL), lambda i: (i, 0))
    vec = pl.BlockSpec((1, D_MODEL), lambda i: (0, 0))
    return pl.pallas_call(
        _outproj_body,
        grid=(m // tm,),
        in_specs=[row, row, row, row, vec, vec, pl.BlockSpec((W_A + W_R, D_MODEL), lambda i: (0, 0))],
        out_specs=row,
        out_shape=jax.ShapeDtypeStruct((m, D_MODEL), F32),
        compiler_params=_params(1),
        name="outproj",
    )(o_att, g_a, y_lru, x2d, norm_attn, norm_final, w_out_bf16)


PROMPT_ROWS = 512
LRU_TILE = 256


def _to_head_major(win):
    b, p = win.shape[0], win.shape[1]
    return jnp.transpose(win, (0, 2, 3, 1)).reshape(b, W_A, p)


def _from_head_major(win_t):
    b, _, p = win_t.shape
    return jnp.transpose(win_t.reshape(b, N_HEADS, HEAD_DIM, p), (0, 3, 1, 2))[None]


def kernel(x_prompt, x_sample, cache_win_k, cache_win_v, state_conv, state_lru, rel_bias, norm_in, w_in, norm_attn, norm_lru, conv_w, conv_b, w_gate_x, b_gate_x, w_gate_a, b_gate_a, lru_param, w_out, norm_final):
    depth = w_in.shape[0]
    assert depth == 1, "single-layer trunk"
    bp, seq, _ = x_prompt.shape
    bs, n_new, _ = x_sample.shape
    wb = cache_win_k.shape[2]
    row = lambda p: p.reshape(1, -1)

    w_in_b = w_in[0].astype(BF16)
    group = lambda i: w_in_b[:, i * W_A:(i + 1) * W_A]
    w_rows = jnp.concatenate([group(0), group(3), group(4), group(5)], axis=1)
    w_kv_t = jnp.concatenate([group(1).T, group(2).T], axis=0)
    w_out_b = w_out[0].astype(BF16)
    w_gates = _gate_weight_tiles(w_gate_x[0], w_gate_a[0])
    lru_args = (conv_w[0], row(conv_b[0]), w_gates, row(b_gate_x[0]), row(b_gate_a[0]), row(lru_param[0]), row(norm_lru[0]))
    bias_prompt, bias_old, bias_new = _bias_tables(rel_bias, n_new, wb)

    q, g_a, x_r, g_r, k_t, v_t = _inproj_prompt(x_prompt, row(norm_in[0]), w_rows, w_kv_t, PROMPT_ROWS)
    xs = x_sample.reshape(bs * n_new, D_MODEL)
    qs, ks, vs, gas, xrs, grs = _inproj_rows(xs, row(norm_in[0]), w_in_b)
    as_new = lambda a: a.reshape(bs, n_new, -1)

    o_att, o_s, new_kt, new_vt = _attn_both(
        q, k_t, v_t, bias_prompt.reshape(N_DIL, N_PAIRS, HEADS_PER_PAIR * BLOCK, 2 * BLOCK),
        as_new(qs), as_new(ks), as_new(vs), _to_head_major(cache_win_k[0]), _to_head_major(cache_win_v[0]),
        bias_old.reshape(N_DIL, N_HEADS * n_new, wb), bias_new.reshape(N_DIL, N_HEADS * n_new, LANES))

    y_prompt, conv_p, lru_p = _lru_outproj(
        x_r, g_r, jnp.zeros((bp, CONV_W - 1, W_R), F32), jnp.zeros((bp, 1, W_R), state_lru.dtype), lru_args,
        o_att, g_a, x_prompt, row(norm_attn[0]), row(norm_final), w_out_b, tt=LRU_TILE)
    y_lru_s, conv_s, lru_s = _lru(as_new(xrs), as_new(grs), state_conv[0], state_lru[0].reshape(bs, 1, W_R),
                                  *lru_args, tt=n_new)
    y_sample = _outproj(o_s.reshape(bs * n_new, W_A), gas, y_lru_s.reshape(bs * n_new, W_R), xs,
                        row(norm_attn[0]), row(norm_final), w_out_b, bs * n_new)

    return (y_prompt.reshape(bp, seq, D_MODEL), y_sample.reshape(bs, n_new, D_MODEL),
            _from_head_major(k_t), _from_head_major(v_t), conv_p[None], lru_p.reshape(1, bp, W_R),
            _from_head_major(new_kt), _from_head_major(new_vt), conv_s[None],
            lru_s.reshape(1, bs, W_R).astype(state_lru.dtype))
```

```python
import functools
import math

import numpy as np
import jax
import jax.numpy as jnp
from jax import lax
from jax.experimental import pallas as pl
from jax.experimental.pallas import tpu as pltpu

D_MODEL = 1024
HEAD_DIM = 64
N_HEADS = 16
W_A = N_HEADS * HEAD_DIM
W_R = D_MODEL
N_LRU_BLOCKS = 16
LRU_BLOCK = W_R // N_LRU_BLOCKS
N_PROJ = 6
CONV_W = 4
LRU_C = 8.0
DILATIONS = (1, 4, 16)
N_DIL = len(DILATIONS)
N_STEPS = 128
BLOCK = 128
N_BUCKETS = 32
MAX_EXACT = 16
MAX_DISTANCE = 2048
EPS = 1e-6
SCALE = HEAD_DIM ** -0.5
NEG_INF = -1e30

LANES = 128
SUBLANES = 8
HEADS_PER_PAIR = LANES // HEAD_DIM
N_PAIRS = N_HEADS // HEADS_PER_PAIR
MXU_TILE = 256
VMEM_LIMIT = 56 * 1024 * 1024

F32 = jnp.float32
BF16 = jnp.bfloat16
NT_DIMS = (((1,), (1,)), ((), ()))


def _params(n_axes, vmem_limit=VMEM_LIMIT):
    return pltpu.CompilerParams(dimension_semantics=("arbitrary",) * n_axes,
                                vmem_limit_bytes=vmem_limit)


def _rms_scale(x):
    return lax.rsqrt(jnp.mean(x * x, axis=-1, keepdims=True) + EPS)


def _resident(shape):
    return pl.BlockSpec(shape, lambda *_: (0,) * len(shape), pipeline_mode=pl.Buffered(1))


def _normed(x_ref, g_ref):
    x = x_ref[...]
    return ((x * _rms_scale(x)) * g_ref[...]).astype(BF16)


def _inproj_rows_body(x_ref, g_ref, w_ref, *out_refs):
    h = _normed(x_ref, g_ref)
    for i, o_ref in enumerate(out_refs):
        r = jnp.dot(h, w_ref[:, i * W_A:(i + 1) * W_A], preferred_element_type=F32)
        if i == 0:
            r = r * SCALE
        o_ref[...] = r


def _inproj_rows(x2d, norm_in, w_in_bf16):
    m = x2d.shape[0]
    row = pl.BlockSpec((m, D_MODEL), lambda i: (0, 0))
    return pl.pallas_call(
        _inproj_rows_body,
        grid=(1,),
        in_specs=[row, pl.BlockSpec((1, D_MODEL), lambda i: (0, 0)),
                  pl.BlockSpec((D_MODEL, N_PROJ * W_A), lambda i: (0, 0))],
        out_specs=[row] * N_PROJ,
        out_shape=[jax.ShapeDtypeStruct((m, W_A), F32)] * N_PROJ,
        compiler_params=_params(1),
        name="inproj_rows",
    )(x2d, norm_in, w_in_bf16)


def _inproj_prompt_body(x_ref, g_ref, wr_ref, wt_ref, q_ref, ga_ref, xr_ref, gr_ref, kt_ref, vt_ref):
    h = _normed(x_ref, g_ref)
    for i, o_ref in enumerate((q_ref, ga_ref, xr_ref, gr_ref)):
        r = jnp.dot(h, wr_ref[:, i * W_A:(i + 1) * W_A], preferred_element_type=F32)
        if i == 0:
            r = r * SCALE
        o_ref[...] = r
    for i, o_ref in enumerate((kt_ref, vt_ref)):
        o_ref[...] = lax.dot_general(wt_ref[i * W_A:(i + 1) * W_A, :], h, NT_DIMS, preferred_element_type=F32)


def _inproj_prompt(x, norm_in, w_rows, w_kv_t, tm):
    b, seq, _ = x.shape
    row = pl.BlockSpec((None, tm, D_MODEL), lambda i, j: (i, j, 0))
    col = pl.BlockSpec((None, W_A, tm), lambda i, j: (i, 0, j))
    rows_shape = jax.ShapeDtypeStruct((b, seq, W_A), F32)
    cols_shape = jax.ShapeDtypeStruct((b, W_A, seq), F32)
    return pl.pallas_call(
        _inproj_prompt_body,
        grid=(b, seq // tm),
        in_specs=[row, _resident((1, D_MODEL)), _resident(w_rows.shape), _resident(w_kv_t.shape)],
        out_specs=[row] * 4 + [col] * 2,
        out_shape=[rows_shape] * 4 + [cols_shape] * 2,
        compiler_params=_params(2),
        name="inproj_prompt",
    )(x, norm_in, w_rows, w_kv_t)


def _t5_bucket_np(dist):
    nf = np.maximum(dist, 1).astype(np.float32)
    large = MAX_EXACT + (np.log(nf / np.float32(MAX_EXACT)) / np.float32(math.log(MAX_DISTANCE / MAX_EXACT))
                         * np.float32(N_BUCKETS - MAX_EXACT)).astype(np.int32)
    large = np.minimum(large, N_BUCKETS - 1)
    return np.where(dist < MAX_EXACT, dist, large).astype(np.int32)


def _bucket_index_tables(n_new, wb):
    i = np.arange(BLOCK)[:, None]
    j = np.arange(2 * BLOCK)[None, :]
    diff = i - j + BLOCK
    band = (diff >= 0) & (diff <= N_STEPS)
    t = np.arange(n_new)[:, None]
    dist_old = wb + t - np.arange(wb)[None, :]
    dist_new = t - np.arange(LANES)[None, :]
    prompt, s_old, s_new = [], [], []
    for dil in DILATIONS:
        prompt.append(np.where(band, _t5_bucket_np(np.maximum(diff, 0) * dil), -1))
        for dist, out, extra in ((dist_old, s_old, True), (dist_new, s_new, np.arange(LANES)[None, :] < n_new)):
            ok = (dist >= 0) & (dist % dil == 0) & (dist // dil <= N_STEPS) & extra
            out.append(np.where(ok, _t5_bucket_np(np.maximum(dist, 0)), -1))
    as_i32 = lambda a: np.stack(a).astype(np.int32)
    return as_i32(prompt), as_i32(s_old), as_i32(s_new)


def _bias_tables_body(rb_ref, *refs, buckets):
    idx_refs, out_refs = refs[:3], refs[3:]
    for d in range(N_DIL):
        idx = [r[d] for r in idx_refs]

        def per_head(h, carry):
            tabs = [jnp.full(ix.shape, NEG_INF, F32) for ix in idx]
            for b in buckets[d]:
                val = rb_ref[b, h]
                tabs = [jnp.where(ix == b, val, tb) for ix, tb in zip(idx, tabs)]
            for o_ref, tb in zip(out_refs, tabs):
                o_ref[d, h] = tb
            return carry
        lax.fori_loop(0, N_HEADS, per_head, 0)


def _bias_tables(rel_bias, n_new, wb):
    idx = _bucket_index_tables(n_new, wb)
    buckets = tuple(tuple(int(b) for b in np.unique(np.concatenate([a[d].ravel() for a in idx])) if b >= 0)
                    for d in range(N_DIL))
    vmem = pl.BlockSpec(memory_space=pltpu.VMEM)
    return pl.pallas_call(
        functools.partial(_bias_tables_body, buckets=buckets),
        in_specs=[pl.BlockSpec(memory_space=pltpu.SMEM)] + [vmem] * 3,
        out_specs=[vmem] * 3,
        out_shape=[jax.ShapeDtypeStruct((N_DIL, N_HEADS) + a.shape[1:], F32) for a in idx],
        compiler_params=pltpu.CompilerParams(vmem_limit_bytes=VMEM_LIMIT),
        name="bias_tables",
    )(rel_bias, *[jnp.asarray(a) for a in idx])


INTERLEAVE = 4


def _prompt_scratch(seq):
    n_blocks = seq // BLOCK
    return ([pltpu.VMEM((seq + BLOCK, LANES), F32)] * 2 + [pltpu.VMEM((seq, LANES), F32)] * 9
            + [pltpu.VMEM((n_blocks, HEADS_PER_PAIR * BLOCK, 2 * BLOCK), BF16)])


def _attn_prompt_body(q_ref, kt_ref, vt_ref, bm_ref, o_ref, k_ref, v_ref, qg_ref, kg_ref, vg_ref,
                      acc16_ref, m16_ref, l16_ref, accp_ref, mp_ref, lp_ref, e_ref, *, seq, extra_work=None):
    group = seq // INTERLEAVE
    nb4 = group // BLOCK
    n_blocks = seq // BLOCK
    assert DILATIONS == (1, INTERLEAVE, INTERLEAVE * INTERLEAVE) and seq == DILATIONS[2] * BLOCK

    first = lax.broadcasted_iota(jnp.int32, (BLOCK, LANES), 1) < HEAD_DIM
    pick = lambda a: jnp.where(first, a[:BLOCK], a[BLOCK:])
    grouped = (qg_ref, kg_ref, vg_ref)
    natural = (q_ref, k_ref, v_ref)

    one_trip = jnp.minimum(pl.program_id(0) + 1, 1)

    def region(fn):
        def body(_, carry):
            fn()
            return carry
        lax.fori_loop(0, one_trip, body, 0)

    def rows_of(start, n, stride):
        return pl.ds(start, n) if stride == 1 else pl.ds(start, n, stride=stride)

    def scores(slot, dil_idx, refs, stride, q_start, k_start, with_prev, mask_prev=False):
        qs_ref, ks_ref, _ = refs
        q = qs_ref[rows_of(q_start, BLOCK, stride), :]
        q2 = jnp.concatenate([jnp.where(first, q, 0.0), jnp.where(first, 0.0, q)], axis=0).astype(BF16)
        nk = 2 * BLOCK if with_prev else BLOCK
        kb = ks_ref[rows_of(k_start, nk, stride), :].astype(BF16)
        bias = bm_ref[dil_idx] if with_prev else bm_ref[dil_idx, :, BLOCK:]
        s = lax.dot_general(q2, kb, NT_DIMS, preferred_element_type=F32) + bias
        if mask_prev:
            prev_cols = lax.broadcasted_iota(jnp.int32, s.shape, 1) < BLOCK
            s = jnp.where(prev_cols, NEG_INF, s)
        m = jnp.max(s, axis=-1, keepdims=True)
        e_ref[slot, :, 0:nk] = jnp.exp(s - m).astype(BF16)
        return pick(m)

    def values(slot, refs, stride, k_start, with_prev):
        nk = 2 * BLOCK if with_prev else BLOCK
        vb = refs[2][rows_of(k_start, nk, stride), :].astype(BF16)
        v1 = jnp.concatenate([vb, jnp.ones((nk, LANES), BF16)], axis=1)
        acc = jnp.dot(e_ref[slot, :, 0:nk], v1, preferred_element_type=F32)
        return pick(acc[:, :LANES]), pick(acc[:, LANES:])

    def merge(parts):
        mx = functools.reduce(jnp.maximum, [m for _, m, _ in parts])
        scale = [jnp.exp(m - mx) for _, m, _ in parts]
        acc = functools.reduce(lambda x, y: x + y, [w * a for w, (a, _, _) in zip(scale, parts)])
        den = functools.reduce(lambda x, y: x + y, [w * l for w, (_, _, l) in zip(scale, parts)])
        return acc, mx, den

    d16_start = [(idx // INTERLEAVE) * group + idx % INTERLEAVE for idx in range(n_blocks)]
    d16_rows = [pl.ds(st, BLOCK, stride=INTERLEAVE) for st in d16_start]
    d4_blocks = [(g * group + n * BLOCK, n > 0) for g in range(INTERLEAVE) for n in range(nb4)]
    d4_rows = [pl.ds(g + n * BLOCK * INTERLEAVE, BLOCK, stride=INTERLEAVE) for g in range(INTERLEAVE) for n in range(nb4)]

    def setup_and_d16_scores():
        for t_ref, r_ref in ((kt_ref, k_ref), (vt_ref, v_ref)):
            r_ref[0:BLOCK, :] = jnp.zeros((BLOCK, LANES), F32)
            r_ref[BLOCK:BLOCK + seq, :] = t_ref[...].T
        for src, dst, off in ((q_ref, qg_ref, 0), (k_ref, kg_ref, BLOCK), (v_ref, vg_ref, BLOCK)):
            for g in range(INTERLEAVE):
                dst[g * group:(g + 1) * group, :] = src[pl.ds(off + g, group, stride=INTERLEAVE), :]
        for slot, st in enumerate(d16_start):
            m16_ref[d16_rows[slot], :] = scores(slot, 2, grouped, INTERLEAVE, st, st, False)

    def d16_values_and_d4_scores():
        for slot, st in enumerate(d16_start):
            acc, l = values(slot, grouped, INTERLEAVE, st, False)
            acc16_ref[d16_rows[slot], :] = acc
            l16_ref[d16_rows[slot], :] = l
        for slot, (st, with_prev) in enumerate(d4_blocks):
            accp_ref[d4_rows[slot], :] = scores(slot, 1, grouped, 1, st, st - BLOCK if with_prev else st, with_prev)

    def d4_values_and_d1_scores():
        for slot, (st, with_prev) in enumerate(d4_blocks):
            acc, l = values(slot, grouped, 1, st - BLOCK if with_prev else st, with_prev)
            rows = pl.ds(st, BLOCK)
            vals = merge([(acc, accp_ref[d4_rows[slot], :], l),
                          (acc16_ref[rows, :], m16_ref[rows, :], l16_ref[rows, :])])
            for ref, val in zip((accp_ref, mp_ref, lp_ref), vals):
                ref[d4_rows[slot], :] = val
        for n in range(n_blocks):
            o_ref[pl.ds(n * BLOCK, BLOCK), :] = scores(n, 0, natural, 1, n * BLOCK, n * BLOCK, True, mask_prev=n == 0)

    def d1_values():
        for n in range(n_blocks):
            rows = pl.ds(n * BLOCK, BLOCK)
            acc, l = values(n, natural, 1, n * BLOCK, True)
            acc, _, den = merge([(acc, o_ref[rows, :], l), (accp_ref[rows, :], mp_ref[rows, :], lp_ref[rows, :])])
            o_ref[rows, :] = acc / den

    stages = (setup_and_d16_scores, d16_values_and_d4_scores, d4_values_and_d1_scores, d1_values)
    for stage, extra in zip(stages, extra_work or (None,) * len(stages)):
        region(stage if extra is None else lambda stage=stage, extra=extra: (extra(), stage()))


def _attn_prompt(q, k_t, v_t, bias_tabs):
    b, seq, _ = q.shape
    rows = pl.BlockSpec((None, seq, LANES), lambda p, i: (i, 0, p))
    cols = pl.BlockSpec((None, LANES, seq), lambda p, i: (i, p, 0))
    scratch = _prompt_scratch(seq)
    return pl.pallas_call(
        functools.partial(_attn_prompt_body, seq=seq),
        grid=(N_PAIRS, b),
        in_specs=[rows, cols, cols,
                  pl.BlockSpec((N_DIL, None, HEADS_PER_PAIR * BLOCK, 2 * BLOCK), lambda p, i: (0, p, 0, 0))],
        out_specs=rows,
        out_shape=jax.ShapeDtypeStruct((b, seq, W_A), F32),
        scratch_shapes=scratch,
        compiler_params=_params(2),
        name="attn_prompt",
    )(q, k_t, v_t, bias_tabs)


SAMPLE_HEADS = 8
SAMPLE_LANES = SAMPLE_HEADS * HEAD_DIM


def _padded_new(new_ref, n_new):
    return jnp.concatenate([new_ref[...], jnp.zeros((LANES - n_new, SAMPLE_LANES), F32)], axis=0)


def _window_update(c_ref, new_ref, out_ref, *, wb, n_new, rows=(0, SAMPLE_LANES)):
    lo, hi = rows
    last = lax.broadcasted_iota(jnp.int32, (hi - lo, LANES), 1) >= LANES - n_new
    rolled = pltpu.roll(c_ref[lo:hi, :], wb - n_new, axis=1)
    out_ref[lo:hi, :] = rolled
    new_t = _padded_new(new_ref, n_new).T[lo:hi, :]
    tail = pltpu.roll(new_t, LANES - n_new, axis=1)
    out_ref[lo:hi, wb - LANES:wb] = jnp.where(last, tail, rolled[:, wb - LANES:wb])


def _attn_sample_body(q_ref, kn_ref, vn_ref, ckt_ref, cvt_ref, bo_ref, bn_ref, o_ref, nkt_ref, nvt_ref,
                      *, wb, n_new):
    _sample_attention(q_ref, kn_ref, vn_ref, ckt_ref, cvt_ref, bo_ref, bn_ref, o_ref, n_new=n_new)
    _window_update(ckt_ref, kn_ref, nkt_ref, wb=wb, n_new=n_new)
    _window_update(cvt_ref, vn_ref, nvt_ref, wb=wb, n_new=n_new)


def _sample_attention(q_ref, kn_ref, vn_ref, ckt_ref, cvt_ref, bo_ref, bn_ref, o_ref, *, n_new):
    rows = SAMPLE_HEADS * n_new
    row_head = lax.broadcasted_iota(jnp.int32, (rows, SAMPLE_LANES), 0) // n_new
    col_head = lax.broadcasted_iota(jnp.int32, (rows, SAMPLE_LANES), 1) // HEAD_DIM
    own_head = row_head == col_head
    q_rep = jnp.concatenate([q_ref[...]] * SAMPLE_HEADS, axis=0)
    q_bd = jnp.where(own_head, q_rep, 0.0).astype(BF16)

    kn = _padded_new(kn_ref, n_new)
    vn = _padded_new(vn_ref, n_new)
    kt = ckt_ref[...]
    vt = cvt_ref[...]

    s_old = jnp.dot(q_bd, kt.astype(BF16), preferred_element_type=F32)
    s_new = lax.dot_general(q_bd, kn.astype(BF16), NT_DIMS, preferred_element_type=F32)
    lo = [s_old + bo_ref[d] for d in range(N_DIL)]
    ln = [s_new + bn_ref[d] for d in range(N_DIL)]
    row_max = lambda t: jnp.max(t, axis=-1, keepdims=True)
    mx = functools.reduce(jnp.maximum, [row_max(t) for t in lo + ln])
    e_old = functools.reduce(lambda a, b: a + b, [jnp.exp(t - mx) for t in lo])
    e_new = functools.reduce(lambda a, b: a + b, [jnp.exp(t - mx) for t in ln])
    den = jnp.sum(e_old, axis=-1, keepdims=True) + jnp.sum(e_new, axis=-1, keepdims=True)
    o_full = lax.dot_general(e_old.astype(BF16), vt.astype(BF16), NT_DIMS, preferred_element_type=F32)
    o_full = o_full + jnp.dot(e_new.astype(BF16), vn.astype(BF16), preferred_element_type=F32)
    o_full = jnp.where(own_head, o_full / den, 0.0)
    out = o_full[0:n_new, :]
    for h in range(1, SAMPLE_HEADS):
        out = out + o_full[h * n_new:(h + 1) * n_new, :]
    o_ref[...] = out


def _attn_sample(q, k_new, v_new, cache_kt, cache_vt, bias_old, bias_new):
    b, n_new, _ = q.shape
    wb = cache_kt.shape[2]
    n_groups = N_HEADS // SAMPLE_HEADS
    new_spec = pl.BlockSpec((None, n_new, SAMPLE_LANES), lambda i, g: (i, 0, g))
    cache_spec = pl.BlockSpec((None, SAMPLE_LANES, wb), lambda i, g: (i, g, 0))
    rows = SAMPLE_HEADS * n_new
    return pl.pallas_call(
        functools.partial(_attn_sample_body, wb=wb, n_new=n_new),
        grid=(b, n_groups),
        in_specs=[new_spec, new_spec, new_spec, cache_spec, cache_spec,
                  pl.BlockSpec((N_DIL, rows, wb), lambda i, g: (0, g, 0)),
                  pl.BlockSpec((N_DIL, rows, LANES), lambda i, g: (0, g, 0))],
        out_specs=[new_spec, cache_spec, cache_spec],
        out_shape=[jax.ShapeDtypeStruct((b, n_new, W_A), F32),
                   jax.ShapeDtypeStruct((b, W_A, wb), F32),
                   jax.ShapeDtypeStruct((b, W_A, wb), F32)],
        compiler_params=_params(2),
        name="attn_sample",
    )(q, k_new, v_new, cache_kt, cache_vt, bias_old, bias_new)


N_PROMPT_IN, N_SAMPLE_IN, N_SAMPLE_OUT = 4, 7, 3
VMEM_LIMIT_BOTH = 62 * 1024 * 1024


def _attn_both_body(*refs, seq, wb, n_new):
    n_in = N_PROMPT_IN + N_SAMPLE_IN
    prompt_in, sample_in = refs[:N_PROMPT_IN], refs[N_PROMPT_IN:n_in]
    o_ref, sample_out, scratch = refs[n_in], refs[n_in + 1:n_in + 1 + N_SAMPLE_OUT], refs[n_in + 1 + N_SAMPLE_OUT:]
    _, kn_ref, vn_ref, ckt_ref, cvt_ref, _, _ = sample_in
    os_ref, nkt_ref, nvt_ref = sample_out
    half = SAMPLE_LANES // 2
    update_k = functools.partial(_window_update, ckt_ref, kn_ref, nkt_ref, wb=wb, n_new=n_new)
    update_v = functools.partial(_window_update, cvt_ref, vn_ref, nvt_ref, wb=wb, n_new=n_new)
    attend = functools.partial(_sample_attention, *sample_in, os_ref, n_new=n_new)
    extra_work = (None, functools.partial(update_k, rows=(0, half)), functools.partial(update_v, rows=(0, half)),
                  lambda: (update_k(rows=(half, SAMPLE_LANES)), update_v(rows=(half, SAMPLE_LANES)), attend()))
    _attn_prompt_body(*prompt_in, o_ref, *scratch, seq=seq, extra_work=extra_work)


def _attn_both(q, k_t, v_t, bias_prompt, qs, k_new, v_new, cache_kt, cache_vt, bias_old, bias_new):
    bp, seq, _ = q.shape
    bs, n_new, _ = qs.shape
    wb = cache_kt.shape[2]
    n_groups = N_HEADS // SAMPLE_HEADS
    assert N_PAIRS * bp == bs * n_groups, "one sample (batch, head group) per prompt (head pair, batch) step"
    step = lambda p, i: p * bp + i
    rows = pl.BlockSpec((None, seq, LANES), lambda p, i: (i, 0, p))
    cols = pl.BlockSpec((None, LANES, seq), lambda p, i: (i, p, 0))
    new_spec = pl.BlockSpec((None, n_new, SAMPLE_LANES), lambda p, i: (step(p, i) // n_groups, 0, step(p, i) % n_groups))
    cache_spec = pl.BlockSpec((None, SAMPLE_LANES, wb), lambda p, i: (step(p, i) // n_groups, step(p, i) % n_groups, 0))
    s_rows = SAMPLE_HEADS * n_new
    scratch = _prompt_scratch(seq)
    return pl.pallas_call(
        functools.partial(_attn_both_body, seq=seq, wb=wb, n_new=n_new),
        grid=(N_PAIRS, bp),
        in_specs=[rows, cols, cols,
                  pl.BlockSpec((N_DIL, None, HEADS_PER_PAIR * BLOCK, 2 * BLOCK), lambda p, i: (0, p, 0, 0)),
                  new_spec, new_spec, new_spec, cache_spec, cache_spec,
                  pl.BlockSpec((N_DIL, s_rows, wb), lambda p, i: (0, step(p, i) % n_groups, 0)),
                  pl.BlockSpec((N_DIL, s_rows, LANES), lambda p, i: (0, step(p, i) % n_groups, 0))],
        out_specs=[rows, new_spec, cache_spec, cache_spec],
        out_shape=[jax.ShapeDtypeStruct((bp, seq, W_A), F32),
                   jax.ShapeDtypeStruct((bs, n_new, W_A), F32),
                   jax.ShapeDtypeStruct((bs, W_A, wb), F32),
                   jax.ShapeDtypeStruct((bs, W_A, wb), F32)],
        scratch_shapes=scratch,
        compiler_params=_params(2, VMEM_LIMIT_BOTH),
        name="attn_both",
    )(q, k_t, v_t, bias_prompt, qs, k_new, v_new, cache_kt, cache_vt, bias_old, bias_new)


def _lru_core(xr_ref, gr_ref, cpast_ref, hpast_ref, cw_ref, cb_ref, wg_ref, bgx_ref, bga_ref, lp_ref, nl_ref,
              nconv_ref, nlru_ref, tail_ref, h_ref, hs_ref, tt):
    t = pl.program_id(1)
    n_t = pl.num_programs(1)

    @pl.when(t == 0)
    def _():
        tail_ref[...] = jnp.concatenate(
            [jnp.zeros((SUBLANES - (CONV_W - 1), W_R), F32), cpast_ref[...]], axis=0)
        h_ref[...] = hpast_ref[...]

    x = xr_ref[...]
    xe = jnp.concatenate([tail_ref[...], x], axis=0)
    xc = cb_ref[...] + cw_ref[CONV_W - 1:CONV_W, :] * x
    for tap in range(CONV_W - 1):
        back = CONV_W - 1 - tap
        xc = xc + cw_ref[tap:tap + 1, :] * xe[SUBLANES - back:SUBLANES - back + tt, :]
    tail_ref[...] = x[tt - SUBLANES:tt, :]

    xcb = xc.astype(BF16)
    gx, ga = [], []
    for j in range(W_R // MXU_TILE):
        g = jnp.dot(xcb[:, j * MXU_TILE:(j + 1) * MXU_TILE], wg_ref[j], preferred_element_type=F32)
        gx.append(g[:, :MXU_TILE])
        ga.append(g[:, MXU_TILE:])
    gate_x = jax.nn.sigmoid(jnp.concatenate(gx, axis=-1) + bgx_ref[...])
    gate_a = jax.nn.sigmoid(jnp.concatenate(ga, axis=-1) + bga_ref[...])
    log_a = (-LRU_C) * gate_a * jax.nn.softplus(-lp_ref[...])
    a = jnp.exp(log_a)
    var = jnp.tanh(-log_a) * (1.0 + a * a)
    root = jnp.where(var > 0.0, var * lax.rsqrt(var), 0.0)
    bx = root * (gate_x * xc)

    groups = tt // SUBLANES
    a3 = a.reshape(groups, SUBLANES, W_R)
    b3 = bx.reshape(groups, SUBLANES, W_R)
    sub = lax.broadcasted_iota(jnp.int32, (groups, SUBLANES, W_R), 1)
    shift = 1
    while shift < SUBLANES:
        keep = sub >= shift
        b_prev = jnp.where(keep, pltpu.roll(b3, shift, axis=1), 0.0)
        a_prev = jnp.where(keep, pltpu.roll(a3, shift, axis=1), 1.0)
        b3 = b3 + a3 * b_prev
        a3 = a3 * a_prev
        shift *= 2
    h = h_ref[...]
    for g in range(groups):
        hg = a3[g] * h + b3[g]
        hs_ref[g * SUBLANES:(g + 1) * SUBLANES, :] = hg
        h = hg[SUBLANES - 1:SUBLANES, :]
    h_ref[...] = h

    @pl.when(t == n_t - 1)
    def _():
        nconv_ref[...] = x[tt - (CONV_W - 1):tt, :]
        nlru_ref[...] = h

    o = hs_ref[...]
    return (((o * _rms_scale(o)) * nl_ref[...]) * jax.nn.silu(gr_ref[...])).astype(BF16)


N_LRU_IN = 11


def _lru_body(*refs, tt):
    ins, (y_ref, nconv_ref, nlru_ref), scratch = refs[:N_LRU_IN], refs[N_LRU_IN:N_LRU_IN + 3], refs[N_LRU_IN + 3:]
    y_ref[...] = _lru_core(*ins, nconv_ref, nlru_ref, *scratch, tt)


def _lru_specs(b, t_len, tt):
    assert t_len % tt == 0 and tt % SUBLANES == 0 and tt >= SUBLANES
    seq_spec = pl.BlockSpec((None, tt, W_R), lambda i, j: (i, j, 0))
    vec = pl.BlockSpec((1, W_R), lambda i, j: (0, 0))
    conv_spec = pl.BlockSpec((None, CONV_W - 1, W_R), lambda i, j: (i, 0, 0))
    h_spec = pl.BlockSpec((None, 1, W_R), lambda i, j: (i, 0, 0))
    in_specs = [seq_spec, seq_spec, conv_spec, h_spec,
                pl.BlockSpec((CONV_W, W_R), lambda i, j: (0, 0)), vec,
                pl.BlockSpec((W_R // MXU_TILE, MXU_TILE, 2 * MXU_TILE), lambda i, j: (0, 0, 0)),
                vec, vec, vec, vec]
    state_shapes = [jax.ShapeDtypeStruct((b, CONV_W - 1, W_R), F32), jax.ShapeDtypeStruct((b, 1, W_R), F32)]
    scratch = [pltpu.VMEM((SUBLANES, W_R), F32), pltpu.VMEM((1, W_R), F32), pltpu.VMEM((tt, W_R), F32)]
    return seq_spec, vec, in_specs, [conv_spec, h_spec], state_shapes, scratch


def _lru(x_r, g_r, conv_past, h_past, conv_w, conv_b, w_gates, b_gate_x, b_gate_a, lru_param, norm_lru, tt):
    b, t_len, _ = x_r.shape
    seq_spec, _, in_specs, state_specs, state_shapes, scratch = _lru_specs(b, t_len, tt)
    return pl.pallas_call(
        functools.partial(_lru_body, tt=tt),
        grid=(b, t_len // tt),
        in_specs=in_specs,
        out_specs=[seq_spec] + state_specs,
        out_shape=[jax.ShapeDtypeStruct((b, t_len, W_R), BF16)] + state_shapes,
        scratch_shapes=scratch,
        compiler_params=_params(2),
        name="rglru",
    )(x_r, g_r, conv_past, h_past, conv_w, conv_b, w_gates, b_gate_x, b_gate_a, lru_param, norm_lru)


def _gate_weight_tiles(w_gate_x, w_gate_a):
    per_tile = MXU_TILE // LRU_BLOCK
    eye = jnp.eye(per_tile, dtype=F32)

    def tiles(w):
        w = w.reshape(N_LRU_BLOCKS // per_tile, per_tile, LRU_BLOCK, LRU_BLOCK)
        bd = jnp.einsum('jaik,ab->jaibk', w, eye)
        return bd.reshape(N_LRU_BLOCKS // per_tile, MXU_TILE, MXU_TILE)
    return jnp.concatenate([tiles(w_gate_x), tiles(w_gate_a)], axis=-1).astype(BF16)


def _outproj_core(o_ref, ga_ref, y_lru, x_ref, na_ref, nf_ref, w_ref):
    o = o_ref[...]
    ya = ((o * _rms_scale(o)) * na_ref[...]) * jax.nn.silu(ga_ref[...])
    acc = jnp.dot(ya.astype(BF16), w_ref[0:W_A, :], preferred_element_type=F32)
    acc = acc + jnp.dot(y_lru, w_ref[W_A:W_A + W_R, :], preferred_element_type=F32)
    y = x_ref[...] + acc
    return (y * _rms_scale(y)) * nf_ref[...]


def _outproj_body(o_ref, ga_ref, yl_ref, x_ref, na_ref, nf_ref, w_ref, y_ref):
    y_ref[...] = _outproj_core(o_ref, ga_ref, yl_ref[...], x_ref, na_ref, nf_ref, w_ref)


N_OUT_IN = 6


def _lru_outproj_body(*refs, tt):
    lru_in, out_in = refs[:N_LRU_IN], refs[N_LRU_IN:N_LRU_IN + N_OUT_IN]
    y_ref, nconv_ref, nlru_ref = refs[N_LRU_IN + N_OUT_IN:N_LRU_IN + N_OUT_IN + 3]
    scratch = refs[N_LRU_IN + N_OUT_IN + 3:]
    o_ref, ga_ref, x_ref, na_ref, nf_ref, w_ref = out_in
    y_lru = _lru_core(*lru_in, nconv_ref, nlru_ref, *scratch, tt)
    y_ref[...] = _outproj_core(o_ref, ga_ref, y_lru, x_ref, na_ref, nf_ref, w_ref)


def _lru_outproj(x_r, g_r, conv_past, h_past, lru_params, o_att, g_a, x, norm_attn, norm_final, w_out_bf16, tt):
    b, t_len, _ = x_r.shape
    seq_spec, vec, in_specs, state_specs, state_shapes, scratch = _lru_specs(b, t_len, tt)
    return pl.pallas_call(
        functools.partial(_lru_outproj_body, tt=tt),
        grid=(b, t_len // tt),
        in_specs=in_specs + [seq_spec, seq_spec, seq_spec, vec, vec, _resident(w_out_bf16.shape)],
        out_specs=[seq_spec] + state_specs,
        out_shape=[jax.ShapeDtypeStruct((b, t_len, D_MODEL), F32)] + state_shapes,
        scratch_shapes=scratch,
        compiler_params=_params(2),
        name="rglru_outproj",
    )(x_r, g_r, conv_past, h_past, *lru_params, o_att, g_a, x, norm_attn, norm_final, w_out_bf16)


def _outproj(o_att, g_a, y_lru, x2d, norm_attn, norm_final, w_out_bf16, tm):
    m = x2d.shape[0]
    row = pl.BlockSpec((tm, D_MODEL), lambda i: (i, 0))
    vec = pl.BlockSpec((1, D_MODEL), lambda i: (0, 0))
    return pl.pallas_call(
        _outproj_body,
        grid=(m // tm,),
        in_specs=[row, row, row, row, vec, vec, pl.BlockSpec((W_A + W_R, D_MODEL), lambda i: (0, 0))],
        out_specs=row,
        out_shape=jax.ShapeDtypeStruct((m, D_MODEL), F32),
        compiler_params=_params(1),
        name="outproj",
    )(o_att, g_a, y_lru, x2d, norm_attn, norm_final, w_out_bf16)


PROMPT_ROWS = 512
LRU_TILE = 256


def _to_head_major(win):
    b, p = win.shape[0], win.shape[1]
    return jnp.transpose(win, (0, 2, 3, 1)).reshape(b, W_A, p)


def _from_head_major(win_t):
    b, _, p = win_t.shape
    return jnp.transpose(win_t.reshape(b, N_HEADS, HEAD_DIM, p), (0, 3, 1, 2))[None]


def kernel(x_prompt, x_sample, cache_win_k, cache_win_v, state_conv, state_lru, rel_bias, norm_in, w_in, norm_attn, norm_lru, conv_w, conv_b, w_gate_x, b_gate_x, w_gate_a, b_gate_a, lru_param, w_out, norm_final):
    depth = w_in.shape[0]
    assert depth == 1, "single-layer trunk"
    bp, seq, _ = x_prompt.shape
    bs, n_new, _ = x_sample.shape
    wb = cache_win_k.shape[2]
    row = lambda p: p.reshape(1, -1)

    w_in_b = w_in[0].astype(BF16)
    group = lambda i: w_in_b[:, i * W_A:(i + 1) * W_A]
    w_rows = jnp.concatenate([group(0), group(3), group(4), group(5)], axis=1)
    w_kv_t = jnp.concatenate([group(1).T, group(2).T], axis=0)
    w_out_b = w_out[0].astype(BF16)
    w_gates = _gate_weight_tiles(w_gate_x[0], w_gate_a[0])
    lru_args = (conv_w[0], row(conv_b[0]), w_gates, row(b_gate_x[0]), row(b_gate_a[0]), row(lru_param[0]), row(norm_lru[0]))
    bias_prompt, bias_old, bias_new = _bias_tables(rel_bias, n_new, wb)

    q, g_a, x_r, g_r, k_t, v_t = _inproj_prompt(x_prompt, row(norm_in[0]), w_rows, w_kv_t, PROMPT_ROWS)
    xs = x_sample.reshape(bs * n_new, D_MODEL)
    qs, ks, vs, gas, xrs, grs = _inproj_rows(xs, row(norm_in[0]), w_in_b)
    as_new = lambda a: a.reshape(bs, n_new, -1)

    o_att, o_s, new_kt, new_vt = _attn_both(
        q, k_t, v_t, bias_prompt.reshape(N_DIL, N_PAIRS, HEADS_PER_PAIR * BLOCK, 2 * BLOCK),
        as_new(qs), as_new(ks), as_new(vs), _to_head_major(cache_win_k[0]), _to_head_major(cache_win_v[0]),
        bias_old.reshape(N_DIL, N_HEADS * n_new, wb), bias_new.reshape(N_DIL, N_HEADS * n_new, LANES))

    y_prompt, conv_p, lru_p = _lru_outproj(
        x_r, g_r, jnp.zeros((bp, CONV_W - 1, W_R), F32), jnp.zeros((bp, 1, W_R), state_lru.dtype), lru_args,
        o_att, g_a, x_prompt, row(norm_attn[0]), row(norm_final), w_out_b, tt=LRU_TILE)
    y_lru_s, conv_s, lru_s = _lru(as_new(xrs), as_new(grs), state_conv[0], state_lru[0].reshape(bs, 1, W_R),
                                  *lru_args, tt=n_new)
    y_sample = _outproj(o_s.reshape(bs * n_new, W_A), gas, y_lru_s.reshape(bs * n_new, W_R), xs,
                        row(norm_attn[0]), row(norm_final), w_out_b, bs * n_new)

    return (y_prompt.reshape(bp, seq, D_MODEL), y_sample.reshape(bs, n_new, D_MODEL),
            _from_head_major(k_t), _from_head_major(v_t), conv_p[None], lru_p.reshape(1, bp, W_R),
            _from_head_major(new_kt), _from_head_major(new_vt), conv_s[None],
            lru_s.reshape(1, bs, W_R).astype(state_lru.dtype))
```

```python
import functools
import math

import numpy as np
import jax
import jax.numpy as jnp
from jax import lax
from jax.experimental import pallas as pl
from jax.experimental.pallas import tpu as pltpu

D_MODEL = 1024
HEAD_DIM = 64
N_HEADS = 16
W_A = N_HEADS * HEAD_DIM
W_R = D_MODEL
N_LRU_BLOCKS = 16
LRU_BLOCK = W_R // N_LRU_BLOCKS
N_PROJ = 6
CONV_W = 4
LRU_C = 8.0
DILATIONS = (1, 4, 16)
N_DIL = len(DILATIONS)
N_STEPS = 128
BLOCK = 128
N_BUCKETS = 32
MAX_EXACT = 16
MAX_DISTANCE = 2048
EPS = 1e-6
SCALE = HEAD_DIM ** -0.5
NEG_INF = -1e30

LANES = 128
SUBLANES = 8
HEADS_PER_PAIR = LANES // HEAD_DIM
N_PAIRS = N_HEADS // HEADS_PER_PAIR
MXU_TILE = 256
VMEM_LIMIT = 56 * 1024 * 1024

F32 = jnp.float32
BF16 = jnp.bfloat16
NT_DIMS = (((1,), (1,)), ((), ()))


def _params(n_axes, vmem_limit=VMEM_LIMIT):
    return pltpu.CompilerParams(dimension_semantics=("arbitrary",) * n_axes,
                                vmem_limit_bytes=vmem_limit)


def _rms_scale(x):
    return lax.rsqrt(jnp.mean(x * x, axis=-1, keepdims=True) + EPS)


def _resident(shape):
    return pl.BlockSpec(shape, lambda *_: (0,) * len(shape), pipeline_mode=pl.Buffered(1))


def _normed(x_ref, g_ref):
    x = x_ref[...]
    return ((x * _rms_scale(x)) * g_ref[...]).astype(BF16)


def _inproj_rows_body(x_ref, g_ref, w_ref, *out_refs):
    h = _normed(x_ref, g_ref)
    for i, o_ref in enumerate(out_refs):
        r = jnp.dot(h, w_ref[:, i * W_A:(i + 1) * W_A], preferred_element_type=F32)
        if i == 0:
            r = r * SCALE
        o_ref[...] = r


def _inproj_rows(x2d, norm_in, w_in_bf16):
    m = x2d.shape[0]
    row = pl.BlockSpec((m, D_MODEL), lambda i: (0, 0))
    return pl.pallas_call(
        _inproj_rows_body,
        grid=(1,),
        in_specs=[row, pl.BlockSpec((1, D_MODEL), lambda i: (0, 0)),
                  pl.BlockSpec((D_MODEL, N_PROJ * W_A), lambda i: (0, 0))],
        out_specs=[row] * N_PROJ,
        out_shape=[jax.ShapeDtypeStruct((m, W_A), F32)] * N_PROJ,
        compiler_params=_params(1),
        name="inproj_rows",
    )(x2d, norm_in, w_in_bf16)


def _inproj_prompt_body(x_ref, g_ref, wr_ref, wt_ref, q_ref, ga_ref, xr_ref, gr_ref, kt_ref, vt_ref):
    h = _normed(x_ref, g_ref)
    for i, o_ref in ((0, q_ref), (3, ga_ref), (4, xr_ref), (5, gr_ref)):
        r = jnp.dot(h, wr_ref[:, i * W_A:(i + 1) * W_A], preferred_element_type=F32)
        if i == 0:
            r = r * SCALE
        o_ref[...] = r
    for i, o_ref in enumerate((kt_ref, vt_ref)):
        o_ref[...] = lax.dot_general(wt_ref[i * W_A:(i + 1) * W_A, :], h, NT_DIMS, preferred_element_type=F32)


def _inproj_prompt(x, norm_in, w_all, w_kv_t, tm):
    b, seq, _ = x.shape
    row = pl.BlockSpec((None, tm, D_MODEL), lambda i, j: (i, j, 0))
    col = pl.BlockSpec((None, W_A, tm), lambda i, j: (i, 0, j))
    rows_shape = jax.ShapeDtypeStruct((b, seq, W_A), F32)
    cols_shape = jax.ShapeDtypeStruct((b, W_A, seq), F32)
    return pl.pallas_call(
        _inproj_prompt_body,
        grid=(b, seq // tm),
        in_specs=[row, _resident((1, D_MODEL)), _resident(w_all.shape), _resident(w_kv_t.shape)],
        out_specs=[row] * 4 + [col] * 2,
        out_shape=[rows_shape] * 4 + [cols_shape] * 2,
        compiler_params=_params(2),
        name="inproj_prompt",
    )(x, norm_in, w_all, w_kv_t)


def _t5_bucket_np(dist):
    nf = np.maximum(dist, 1).astype(np.float32)
    large = MAX_EXACT + (np.log(nf / np.float32(MAX_EXACT)) / np.float32(math.log(MAX_DISTANCE / MAX_EXACT))
                         * np.float32(N_BUCKETS - MAX_EXACT)).astype(np.int32)
    large = np.minimum(large, N_BUCKETS - 1)
    return np.where(dist < MAX_EXACT, dist, large).astype(np.int32)


def _bucket_index_tables(n_new, wb):
    i = np.arange(BLOCK)[:, None]
    j = np.arange(2 * BLOCK)[None, :]
    diff = i - j + BLOCK
    band = (diff >= 0) & (diff <= N_STEPS)
    t = np.arange(n_new)[:, None]
    dist_old = wb + t - np.arange(wb)[None, :]
    dist_new = t - np.arange(LANES)[None, :]
    prompt, s_old, s_new = [], [], []
    for dil in DILATIONS:
        prompt.append(np.where(band, _t5_bucket_np(np.maximum(diff, 0) * dil), -1))
        for dist, out, extra in ((dist_old, s_old, True), (dist_new, s_new, np.arange(LANES)[None, :] < n_new)):
            ok = (dist >= 0) & (dist % dil == 0) & (dist // dil <= N_STEPS) & extra
            out.append(np.where(ok, _t5_bucket_np(np.maximum(dist, 0)), -1))
    as_i32 = lambda a: np.stack(a).astype(np.int32)
    return as_i32(prompt), as_i32(s_old), as_i32(s_new)


def _bias_tables_body(rb_ref, *refs, buckets):
    idx_refs, out_refs = refs[:3], refs[3:]
    for d in range(N_DIL):
        idx = [r[d] for r in idx_refs]

        def per_head(h, carry):
            tabs = [jnp.full(ix.shape, NEG_INF, F32) for ix in idx]
            for b in buckets[d]:
                val = rb_ref[b, h]
                tabs = [jnp.where(ix == b, val, tb) for ix, tb in zip(idx, tabs)]
            for o_ref, tb in zip(out_refs, tabs):
                o_ref[d, h] = tb
            return carry
        lax.fori_loop(0, N_HEADS, per_head, 0)


def _bias_tables(rel_bias, n_new, wb):
    idx = _bucket_index_tables(n_new, wb)
    buckets = tuple(tuple(int(b) for b in np.unique(np.concatenate([a[d].ravel() for a in idx])) if b >= 0)
                    for d in range(N_DIL))
    vmem = pl.BlockSpec(memory_space=pltpu.VMEM)
    return pl.pallas_call(
        functools.partial(_bias_tables_body, buckets=buckets),
        in_specs=[pl.BlockSpec(memory_space=pltpu.SMEM)] + [vmem] * 3,
        out_specs=[vmem] * 3,
        out_shape=[jax.ShapeDtypeStruct((N_DIL, N_HEADS) + a.shape[1:], F32) for a in idx],
        compiler_params=pltpu.CompilerParams(vmem_limit_bytes=VMEM_LIMIT),
        name="bias_tables",
    )(rel_bias, *[jnp.asarray(a) for a in idx])


INTERLEAVE = 4


def _prompt_scratch(seq):
    n_blocks = seq // BLOCK
    return ([pltpu.VMEM((seq + BLOCK, LANES), F32)] * 2 + [pltpu.VMEM((seq, LANES), F32)] * 9
            + [pltpu.VMEM((n_blocks, HEADS_PER_PAIR * BLOCK, 2 * BLOCK), BF16)])


def _attn_prompt_body(q_ref, kt_ref, vt_ref, bm_ref, o_ref, k_ref, v_ref, qg_ref, kg_ref, vg_ref,
                      acc16_ref, m16_ref, l16_ref, accp_ref, mp_ref, lp_ref, e_ref, *, seq, extra_work=None):
    group = seq // INTERLEAVE
    nb4 = group // BLOCK
    n_blocks = seq // BLOCK
    assert DILATIONS == (1, INTERLEAVE, INTERLEAVE * INTERLEAVE) and seq == DILATIONS[2] * BLOCK

    first = lax.broadcasted_iota(jnp.int32, (BLOCK, LANES), 1) < HEAD_DIM
    pick = lambda a: jnp.where(first, a[:BLOCK], a[BLOCK:])
    grouped = (qg_ref, kg_ref, vg_ref)
    natural = (q_ref, k_ref, v_ref)

    one_trip = jnp.minimum(pl.program_id(0) + 1, 1)

    def region(fn):
        def body(_, carry):
            fn()
            return carry
        lax.fori_loop(0, one_trip, body, 0)

    def rows_of(start, n, stride):
        return pl.ds(start, n) if stride == 1 else pl.ds(start, n, stride=stride)

    def scores(slot, dil_idx, refs, stride, q_start, k_start, with_prev, mask_prev=False):
        qs_ref, ks_ref, _ = refs
        q = qs_ref[rows_of(q_start, BLOCK, stride), :]
        q2 = jnp.concatenate([jnp.where(first, q, 0.0), jnp.where(first, 0.0, q)], axis=0).astype(BF16)
        nk = 2 * BLOCK if with_prev else BLOCK
        kb = ks_ref[rows_of(k_start, nk, stride), :].astype(BF16)
        bias = bm_ref[dil_idx] if with_prev else bm_ref[dil_idx, :, BLOCK:]
        s = lax.dot_general(q2, kb, NT_DIMS, preferred_element_type=F32) + bias
        if mask_prev:
            prev_cols = lax.broadcasted_iota(jnp.int32, s.shape, 1) < BLOCK
            s = jnp.where(prev_cols, NEG_INF, s)
        m = jnp.max(s, axis=-1, keepdims=True)
        e_ref[slot, :, 0:nk] = jnp.exp(s - m).astype(BF16)
        return pick(m)

    def values(slot, refs, stride, k_start, with_prev):
        nk = 2 * BLOCK if with_prev else BLOCK
        vb = refs[2][rows_of(k_start, nk, stride), :].astype(BF16)
        v1 = jnp.concatenate([vb, jnp.ones((nk, LANES), BF16)], axis=1)
        acc = jnp.dot(e_ref[slot, :, 0:nk], v1, preferred_element_type=F32)
        return pick(acc[:, :LANES]), pick(acc[:, LANES:])

    def merge(parts):
        mx = functools.reduce(jnp.maximum, [m for _, m, _ in parts])
        scale = [jnp.exp(m - mx) for _, m, _ in parts]
        acc = functools.reduce(lambda x, y: x + y, [w * a for w, (a, _, _) in zip(scale, parts)])
        den = functools.reduce(lambda x, y: x + y, [w * l for w, (_, _, l) in zip(scale, parts)])
        return acc, mx, den

    d16_start = [(idx // INTERLEAVE) * group + idx % INTERLEAVE for idx in range(n_blocks)]
    d16_rows = [pl.ds(st, BLOCK, stride=INTERLEAVE) for st in d16_start]
    d4_blocks = [(g * group + n * BLOCK, n > 0) for g in range(INTERLEAVE) for n in range(nb4)]
    d4_rows = [pl.ds(g + n * BLOCK * INTERLEAVE, BLOCK, stride=INTERLEAVE) for g in range(INTERLEAVE) for n in range(nb4)]

    def setup_and_d16_scores():
        def rows_from(t_ref, r_ref):
            r_ref[0:BLOCK, :] = jnp.zeros((BLOCK, LANES), F32)
            r_ref[BLOCK:BLOCK + seq, :] = t_ref[...].T

        def group_rows(src, dst, off):
            for g in range(INTERLEAVE):
                dst[g * group:(g + 1) * group, :] = src[pl.ds(off + g, group, stride=INTERLEAVE), :]

        rows_from(kt_ref, k_ref)
        rows_from(vt_ref, v_ref)
        group_rows(q_ref, qg_ref, 0)
        group_rows(k_ref, kg_ref, BLOCK)
        group_rows(v_ref, vg_ref, BLOCK)
        for slot, st in enumerate(d16_start):
            m16_ref[d16_rows[slot], :] = scores(slot, 2, grouped, INTERLEAVE, st, st, False)

    def d16_values_and_d4_scores():
        for slot, st in enumerate(d16_start):
            acc, l = values(slot, grouped, INTERLEAVE, st, False)
            acc16_ref[d16_rows[slot], :] = acc
            l16_ref[d16_rows[slot], :] = l
        for slot, (st, with_prev) in enumerate(d4_blocks):
            accp_ref[d4_rows[slot], :] = scores(slot, 1, grouped, 1, st, st - BLOCK if with_prev else st, with_prev)

    def d4_values_and_d1_scores():
        for slot, (st, with_prev) in enumerate(d4_blocks):
            acc, l = values(slot, grouped, 1, st - BLOCK if with_prev else st, with_prev)
            rows = pl.ds(st, BLOCK)
            vals = merge([(acc, accp_ref[d4_rows[slot], :], l),
                          (acc16_ref[rows, :], m16_ref[rows, :], l16_ref[rows, :])])
            for ref, val in zip((accp_ref, mp_ref, lp_ref), vals):
                ref[d4_rows[slot], :] = val
        for n in range(n_blocks):
            o_ref[pl.ds(n * BLOCK, BLOCK), :] = scores(n, 0, natural, 1, n * BLOCK, n * BLOCK, True, mask_prev=n == 0)

    def d1_values():
        for n in range(n_blocks):
            rows = pl.ds(n * BLOCK, BLOCK)
            acc, l = values(n, natural, 1, n * BLOCK, True)
            acc, _, den = merge([(acc, o_ref[rows, :], l), (accp_ref[rows, :], mp_ref[rows, :], lp_ref[rows, :])])
            o_ref[rows, :] = acc / den

    stages = (setup_and_d16_scores, d16_values_and_d4_scores, d4_values_and_d1_scores, d1_values)
    for stage, extra in zip(stages, extra_work or (None,) * len(stages)):
        region(stage if extra is None else lambda stage=stage, extra=extra: (extra(), stage()))


def _attn_prompt(q, k_t, v_t, bias_tabs):
    b, seq, _ = q.shape
    rows = pl.BlockSpec((None, seq, LANES), lambda p, i: (i, 0, p))
    cols = pl.BlockSpec((None, LANES, seq), lambda p, i: (i, p, 0))
    scratch = _prompt_scratch(seq)
    return pl.pallas_call(
        functools.partial(_attn_prompt_body, seq=seq),
        grid=(N_PAIRS, b),
        in_specs=[rows, cols, cols,
                  pl.BlockSpec((N_DIL, None, HEADS_PER_PAIR * BLOCK, 2 * BLOCK), lambda p, i: (0, p, 0, 0))],
        out_specs=rows,
        out_shape=jax.ShapeDtypeStruct((b, seq, W_A), F32),
        scratch_shapes=scratch,
        compiler_params=_params(2),
        name="attn_prompt",
    )(q, k_t, v_t, bias_tabs)


SAMPLE_HEADS = 8
SAMPLE_LANES = SAMPLE_HEADS * HEAD_DIM


def _padded_new(new_ref, n_new):
    return jnp.concatenate([new_ref[...], jnp.zeros((LANES - n_new, SAMPLE_LANES), F32)], axis=0)


def _window_update(c_ref, new_ref, out_ref, *, wb, n_new, rows=(0, SAMPLE_LANES)):
    lo, hi = rows
    last = lax.broadcasted_iota(jnp.int32, (hi - lo, LANES), 1) >= LANES - n_new
    rolled = pltpu.roll(c_ref[lo:hi, :], wb - n_new, axis=1)
    out_ref[lo:hi, :] = rolled
    new_t = _padded_new(new_ref, n_new).T[lo:hi, :]
    tail = pltpu.roll(new_t, LANES - n_new, axis=1)
    out_ref[lo:hi, wb - LANES:wb] = jnp.where(last, tail, rolled[:, wb - LANES:wb])


def _attn_sample_body(q_ref, kn_ref, vn_ref, ckt_ref, cvt_ref, bo_ref, bn_ref, o_ref, nkt_ref, nvt_ref,
                      *, wb, n_new):
    _sample_attention(q_ref, kn_ref, vn_ref, ckt_ref, cvt_ref, bo_ref, bn_ref, o_ref, n_new=n_new)
    _window_update(ckt_ref, kn_ref, nkt_ref, wb=wb, n_new=n_new)
    _window_update(cvt_ref, vn_ref, nvt_ref, wb=wb, n_new=n_new)


def _sample_attention(q_ref, kn_ref, vn_ref, ckt_ref, cvt_ref, bo_ref, bn_ref, o_ref, *, n_new):
    rows = SAMPLE_HEADS * n_new
    row_head = lax.broadcasted_iota(jnp.int32, (rows, SAMPLE_LANES), 0) // n_new
    col_head = lax.broadcasted_iota(jnp.int32, (rows, SAMPLE_LANES), 1) // HEAD_DIM
    own_head = row_head == col_head
    q_rep = jnp.concatenate([q_ref[...]] * SAMPLE_HEADS, axis=0)
    q_bd = jnp.where(own_head, q_rep, 0.0).astype(BF16)

    kn = _padded_new(kn_ref, n_new)
    vn = _padded_new(vn_ref, n_new)
    kt = ckt_ref[...]
    vt = cvt_ref[...]

    s_old = jnp.dot(q_bd, kt.astype(BF16), preferred_element_type=F32)
    s_new = lax.dot_general(q_bd, kn.astype(BF16), NT_DIMS, preferred_element_type=F32)
    lo = [s_old + bo_ref[d] for d in range(N_DIL)]
    ln = [s_new + bn_ref[d] for d in range(N_DIL)]
    row_max = lambda t: jnp.max(t, axis=-1, keepdims=True)
    mx = functools.reduce(jnp.maximum, [row_max(t) for t in lo + ln])
    e_old = functools.reduce(lambda a, b: a + b, [jnp.exp(t - mx) for t in lo])
    e_new = functools.reduce(lambda a, b: a + b, [jnp.exp(t - mx) for t in ln])
    den = jnp.sum(e_old, axis=-1, keepdims=True) + jnp.sum(e_new, axis=-1, keepdims=True)
    o_full = lax.dot_general(e_old.astype(BF16), vt.astype(BF16), NT_DIMS, preferred_element_type=F32)
    o_full = o_full + jnp.dot(e_new.astype(BF16), vn.astype(BF16), preferred_element_type=F32)
    o_full = jnp.where(own_head, o_full / den, 0.0)
    out = o_full[0:n_new, :]
    for h in range(1, SAMPLE_HEADS):
        out = out + o_full[h * n_new:(h + 1) * n_new, :]
    o_ref[...] = out


def _attn_sample(q, k_new, v_new, cache_kt, cache_vt, bias_old, bias_new):
    b, n_new, _ = q.shape
    wb = cache_kt.shape[2]
    n_groups = N_HEADS // SAMPLE_HEADS
    new_spec = pl.BlockSpec((None, n_new, SAMPLE_LANES), lambda i, g: (i, 0, g))
    cache_spec = pl.BlockSpec((None, SAMPLE_LANES, wb), lambda i, g: (i, g, 0))
    rows = SAMPLE_HEADS * n_new
    return pl.pallas_call(
        functools.partial(_attn_sample_body, wb=wb, n_new=n_new),
        grid=(b, n_groups),
        in_specs=[new_spec, new_spec, new_spec, cache_spec, cache_spec,
                  pl.BlockSpec((N_DIL, rows, wb), lambda i, g: (0, g, 0)),
                  pl.BlockSpec((N_DIL, rows, LANES), lambda i, g: (0, g, 0))],
        out_specs=[new_spec, cache_spec, cache_spec],
        out_shape=[jax.ShapeDtypeStruct((b, n_new, W_A), F32),
                   jax.ShapeDtypeStruct((b, W_A, wb), F32),
                   jax.ShapeDtypeStruct((b, W_A, wb), F32)],
        compiler_params=_params(2),
        name="attn_sample",
    )(q, k_new, v_new, cache_kt, cache_vt, bias_old, bias_new)


N_PROMPT_IN, N_SAMPLE_IN, N_SAMPLE_OUT = 4, 7, 3
VMEM_LIMIT_BOTH = 62 * 1024 * 1024


def _attn_both_body(*refs, seq, wb, n_new):
    n_in = N_PROMPT_IN + N_SAMPLE_IN
    prompt_in, sample_in = refs[:N_PROMPT_IN], refs[N_PROMPT_IN:n_in]
    o_ref, sample_out, scratch = refs[n_in], refs[n_in + 1:n_in + 1 + N_SAMPLE_OUT], refs[n_in + 1 + N_SAMPLE_OUT:]
    _, kn_ref, vn_ref, ckt_ref, cvt_ref, _, _ = sample_in
    os_ref, nkt_ref, nvt_ref = sample_out
    half = SAMPLE_LANES // 2
    update_k = functools.partial(_window_update, ckt_ref, kn_ref, nkt_ref, wb=wb, n_new=n_new)
    update_v = functools.partial(_window_update, cvt_ref, vn_ref, nvt_ref, wb=wb, n_new=n_new)
    attend = functools.partial(_sample_attention, *sample_in, os_ref, n_new=n_new)
    extra_work = (None, functools.partial(update_k, rows=(0, half)), functools.partial(update_v, rows=(0, half)),
                  lambda: (update_k(rows=(half, SAMPLE_LANES)), update_v(rows=(half, SAMPLE_LANES)), attend()))
    _attn_prompt_body(*prompt_in, o_ref, *scratch, seq=seq, extra_work=extra_work)


def _attn_both(q, k_t, v_t, bias_prompt, qs, k_new, v_new, cache_kt, cache_vt, bias_old, bias_new):
    bp, seq, _ = q.shape
    bs, n_new, _ = qs.shape
    wb = cache_kt.shape[2]
    n_groups = N_HEADS // SAMPLE_HEADS
    assert N_PAIRS * bp == bs * n_groups, "one sample (batch, head group) per prompt (head pair, batch) step"
    step = lambda p, i: p * bp + i
    rows = pl.BlockSpec((None, seq, LANES), lambda p, i: (i, 0, p))
    cols = pl.BlockSpec((None, LANES, seq), lambda p, i: (i, p, 0))
    new_spec = pl.BlockSpec((None, n_new, SAMPLE_LANES), lambda p, i: (step(p, i) // n_groups, 0, step(p, i) % n_groups))
    cache_spec = pl.BlockSpec((None, SAMPLE_LANES, wb), lambda p, i: (step(p, i) // n_groups, step(p, i) % n_groups, 0))
    s_rows = SAMPLE_HEADS * n_new
    scratch = _prompt_scratch(seq)
    return pl.pallas_call(
        functools.partial(_attn_both_body, seq=seq, wb=wb, n_new=n_new),
        grid=(N_PAIRS, bp),
        in_specs=[rows, cols, cols,
                  pl.BlockSpec((N_DIL, None, HEADS_PER_PAIR * BLOCK, 2 * BLOCK), lambda p, i: (0, p, 0, 0)),
                  new_spec, new_spec, new_spec, cache_spec, cache_spec,
                  pl.BlockSpec((N_DIL, s_rows, wb), lambda p, i: (0, step(p, i) % n_groups, 0)),
                  pl.BlockSpec((N_DIL, s_rows, LANES), lambda p, i: (0, step(p, i) % n_groups, 0))],
        out_specs=[rows, new_spec, cache_spec, cache_spec],
        out_shape=[jax.ShapeDtypeStruct((bp, seq, W_A), F32),
                   jax.ShapeDtypeStruct((bs, n_new, W_A), F32),
                   jax.ShapeDtypeStruct((bs, W_A, wb), F32),
                   jax.ShapeDtypeStruct((bs, W_A, wb), F32)],
        scratch_shapes=scratch,
        compiler_params=_params(2, VMEM_LIMIT_BOTH),
        name="attn_both",
    )(q, k_t, v_t, bias_prompt, qs, k_new, v_new, cache_kt, cache_vt, bias_old, bias_new)


def _lru_core(xr_ref, gr_ref, cpast_ref, hpast_ref, cw_ref, cb_ref, wg_ref, bgx_ref, bga_ref, lp_ref, nl_ref,
              nconv_ref, nlru_ref, tail_ref, h_ref, hs_ref, tt):
    t = pl.program_id(1)
    n_t = pl.num_programs(1)

    @pl.when(t == 0)
    def _():
        tail_ref[...] = jnp.concatenate(
            [jnp.zeros((SUBLANES - (CONV_W - 1), W_R), F32), cpast_ref[...]], axis=0)
        h_ref[...] = hpast_ref[...]

    x = xr_ref[...]
    xe = jnp.concatenate([tail_ref[...], x], axis=0)
    xc = cb_ref[...] + cw_ref[CONV_W - 1:CONV_W, :] * x
    for tap in range(CONV_W - 1):
        back = CONV_W - 1 - tap
        xc = xc + cw_ref[tap:tap + 1, :] * xe[SUBLANES - back:SUBLANES - back + tt, :]
    tail_ref[...] = x[tt - SUBLANES:tt, :]

    xcb = xc.astype(BF16)
    gx, ga = [], []
    for j in range(W_R // MXU_TILE):
        g = jnp.dot(xcb[:, j * MXU_TILE:(j + 1) * MXU_TILE], wg_ref[j], preferred_element_type=F32)
        gx.append(g[:, :MXU_TILE])
        ga.append(g[:, MXU_TILE:])
    gate_x = jax.nn.sigmoid(jnp.concatenate(gx, axis=-1) + bgx_ref[...])
    gate_a = jax.nn.sigmoid(jnp.concatenate(ga, axis=-1) + bga_ref[...])
    log_a = (-LRU_C) * gate_a * jax.nn.softplus(-lp_ref[...])
    a = jnp.exp(log_a)
    var = jnp.tanh(-log_a) * (1.0 + a * a)
    root = jnp.where(var > 0.0, var * lax.rsqrt(var), 0.0)
    bx = root * (gate_x * xc)

    groups = tt // SUBLANES
    a3 = a.reshape(groups, SUBLANES, W_R)
    b3 = bx.reshape(groups, SUBLANES, W_R)
    sub = lax.broadcasted_iota(jnp.int32, (groups, SUBLANES, W_R), 1)
    shift = 1
    while shift < SUBLANES:
        keep = sub >= shift
        b_prev = jnp.where(keep, pltpu.roll(b3, shift, axis=1), 0.0)
        a_prev = jnp.where(keep, pltpu.roll(a3, shift, axis=1), 1.0)
        b3 = b3 + a3 * b_prev
        a3 = a3 * a_prev
        shift *= 2
    h = h_ref[...]
    for g in range(groups):
        hg = a3[g] * h + b3[g]
        hs_ref[g * SUBLANES:(g + 1) * SUBLANES, :] = hg
        h = hg[SUBLANES - 1:SUBLANES, :]
    h_ref[...] = h

    @pl.when(t == n_t - 1)
    def _():
        nconv_ref[...] = x[tt - (CONV_W - 1):tt, :]
        nlru_ref[...] = h

    o = hs_ref[...]
    return (((o * _rms_scale(o)) * nl_ref[...]) * jax.nn.silu(gr_ref[...])).astype(BF16)


N_LRU_IN = 11


N_LRU_PER_BATCH = 4


def _lru_body(*refs, tt, nb):
    ins, (y_ref, nconv_ref, nlru_ref), scratch = refs[:N_LRU_IN], refs[N_LRU_IN:N_LRU_IN + 3], refs[N_LRU_IN + 3:]
    if nb is None:
        y_ref[...] = _lru_core(*ins, nconv_ref, nlru_ref, *scratch, tt)
        return
    for bi in range(nb):
        ins_b = [r.at[bi] if k < N_LRU_PER_BATCH else r for k, r in enumerate(ins)]
        y_ref[bi] = _lru_core(*ins_b, nconv_ref.at[bi], nlru_ref.at[bi], *scratch, tt)


def _lru_specs(b, t_len, tt, nb=None):
    assert t_len % tt == 0 and tt % SUBLANES == 0 and tt >= SUBLANES
    assert nb is None or (t_len == tt and b % nb == 0)
    seq_spec = pl.BlockSpec((nb, tt, W_R), lambda i, j: (i, j, 0))
    vec = pl.BlockSpec((1, W_R), lambda i, j: (0, 0))
    conv_spec = pl.BlockSpec((nb, CONV_W - 1, W_R), lambda i, j: (i, 0, 0))
    h_spec = pl.BlockSpec((nb, 1, W_R), lambda i, j: (i, 0, 0))
    in_specs = [seq_spec, seq_spec, conv_spec, h_spec,
                pl.BlockSpec((CONV_W, W_R), lambda i, j: (0, 0)), vec,
                pl.BlockSpec((W_R // MXU_TILE, MXU_TILE, 2 * MXU_TILE), lambda i, j: (0, 0, 0)),
                vec, vec, vec, vec]
    state_shapes = [jax.ShapeDtypeStruct((b, CONV_W - 1, W_R), F32), jax.ShapeDtypeStruct((b, 1, W_R), F32)]
    scratch = [pltpu.VMEM((SUBLANES, W_R), F32), pltpu.VMEM((1, W_R), F32), pltpu.VMEM((tt, W_R), F32)]
    return seq_spec, vec, in_specs, [conv_spec, h_spec], state_shapes, scratch


def _lru(x_r, g_r, conv_past, h_past, conv_w, conv_b, w_gates, b_gate_x, b_gate_a, lru_param, norm_lru, tt, nb=None):
    b, t_len, _ = x_r.shape
    seq_spec, _, in_specs, state_specs, state_shapes, scratch = _lru_specs(b, t_len, tt, nb)
    return pl.pallas_call(
        functools.partial(_lru_body, tt=tt, nb=nb),
        grid=(b // (nb or 1), t_len // tt),
        in_specs=in_specs,
        out_specs=[seq_spec] + state_specs,
        out_shape=[jax.ShapeDtypeStruct((b, t_len, W_R), BF16)] + state_shapes,
        scratch_shapes=scratch,
        compiler_params=_params(2),
        name="rglru",
    )(x_r, g_r, conv_past, h_past, conv_w, conv_b, w_gates, b_gate_x, b_gate_a, lru_param, norm_lru)


def _gate_weight_tiles(w_gate_x, w_gate_a):
    per_tile = MXU_TILE // LRU_BLOCK
    eye = jnp.eye(per_tile, dtype=F32)

    def tiles(w):
        w = w.reshape(N_LRU_BLOCKS // per_tile, per_tile, LRU_BLOCK, LRU_BLOCK)
        bd = jnp.einsum('jaik,ab->jaibk', w, eye)
        return bd.reshape(N_LRU_BLOCKS // per_tile, MXU_TILE, MXU_TILE)
    return jnp.concatenate([tiles(w_gate_x), tiles(w_gate_a)], axis=-1).astype(BF16)


def _outproj_core(o_ref, ga_ref, y_lru, x_ref, na_ref, nf_ref, w_ref):
    o = o_ref[...]
    ya = ((o * _rms_scale(o)) * na_ref[...]) * jax.nn.silu(ga_ref[...])
    acc = jnp.dot(ya.astype(BF16), w_ref[0:W_A, :], preferred_element_type=F32)
    acc = acc + jnp.dot(y_lru, w_ref[W_A:W_A + W_R, :], preferred_element_type=F32)
    y = x_ref[...] + acc
    return (y * _rms_scale(y)) * nf_ref[...]


def _outproj_body(o_ref, ga_ref, yl_ref, x_ref, na_ref, nf_ref, w_ref, y_ref):
    y_ref[...] = _outproj_core(o_ref, ga_ref, yl_ref[...], x_ref, na_ref, nf_ref, w_ref)


N_OUT_IN = 6


def _lru_outproj_body(*refs, tt):
    lru_in, out_in = refs[:N_LRU_IN], refs[N_LRU_IN:N_LRU_IN + N_OUT_IN]
    y_ref, nconv_ref, nlru_ref = refs[N_LRU_IN + N_OUT_IN:N_LRU_IN + N_OUT_IN + 3]
    scratch = refs[N_LRU_IN + N_OUT_IN + 3:]
    o_ref, ga_ref, x_ref, na_ref, nf_ref, w_ref = out_in
    y_lru = _lru_core(*lru_in, nconv_ref, nlru_ref, *scratch, tt)
    y_ref[...] = _outproj_core(o_ref, ga_ref, y_lru, x_ref, na_ref, nf_ref, w_ref)


def _lru_outproj(x_r, g_r, conv_past, h_past, lru_params, o_att, g_a, x, norm_attn, norm_final, w_out_bf16, tt):
    b, t_len, _ = x_r.shape
    seq_spec, vec, in_specs, state_specs, state_shapes, scratch = _lru_specs(b, t_len, tt)
    return pl.pallas_call(
        functools.partial(_lru_outproj_body, tt=tt),
        grid=(b, t_len // tt),
        in_specs=in_specs + [seq_spec, seq_spec, seq_spec, vec, vec, _resident(w_out_bf16.shape)],
        out_specs=[seq_spec] + state_specs,
        out_shape=[jax.ShapeDtypeStruct((b, t_len, D_MODEL), F32)] + state_shapes,
        scratch_shapes=scratch,
        compiler_params=_params(2),
        name="rglru_outproj",
    )(x_r, g_r, conv_past, h_past, *lru_params, o_att, g_a, x, norm_attn, norm_final, w_out_bf16)


def _outproj(o_att, g_a, y_lru, x2d, norm_attn, norm_final, w_out_bf16, tm):
    m = x2d.shape[0]
    row = pl.BlockSpec((tm, D_MODEL), lambda i: (i, 0))
    vec = pl.BlockSpec((1, D_MODEL), lambda i: (0, 0))
    return pl.pallas_call(
        _outproj_body,
        grid=(m // tm,),
        in_specs=[row, row, row, row, vec, vec, pl.BlockSpec((W_A + W_R, D_MODEL), lambda i: (0, 0))],
        out_specs=row,
        out_shape=jax.ShapeDtypeStruct((m, D_MODEL), F32),
        compiler_params=_params(1),
        name="outproj",
    )(o_att, g_a, y_lru, x2d, norm_attn, norm_final, w_out_bf16)


PROMPT_ROWS = 512
LRU_TILE = 512
SAMPLE_LRU_BATCH = 8


def _to_head_major(win):
    b, p = win.shape[0], win.shape[1]
    return jnp.transpose(win, (0, 2, 3, 1)).reshape(b, W_A, p)


def _from_head_major(win_t):
    b, _, p = win_t.shape
    return jnp.transpose(win_t.reshape(b, N_HEADS, HEAD_DIM, p), (0, 3, 1, 2))[None]


def kernel(x_prompt, x_sample, cache_win_k, cache_win_v, state_conv, state_lru, rel_bias, norm_in, w_in, norm_attn, norm_lru, conv_w, conv_b, w_gate_x, b_gate_x, w_gate_a, b_gate_a, lru_param, w_out, norm_final):
    depth = w_in.shape[0]
    assert depth == 1, "single-layer trunk"
    bp, seq, _ = x_prompt.shape
    bs, n_new, _ = x_sample.shape
    wb = cache_win_k.shape[2]
    row = lambda p: p.reshape(1, -1)

    w_in_b = w_in[0].astype(BF16)
    w_kv_t = w_in_b[:, W_A:3 * W_A].T
    w_out_b = w_out[0].astype(BF16)
    w_gates = _gate_weight_tiles(w_gate_x[0], w_gate_a[0])
    lru_args = (conv_w[0], row(conv_b[0]), w_gates, row(b_gate_x[0]), row(b_gate_a[0]), row(lru_param[0]), row(norm_lru[0]))
    bias_prompt, bias_old, bias_new = _bias_tables(rel_bias, n_new, wb)

    q, g_a, x_r, g_r, k_t, v_t = _inproj_prompt(x_prompt, row(norm_in[0]), w_in_b, w_kv_t, PROMPT_ROWS)
    xs = x_sample.reshape(bs * n_new, D_MODEL)
    qs, ks, vs, gas, xrs, grs = _inproj_rows(xs, row(norm_in[0]), w_in_b)
    as_new = lambda a: a.reshape(bs, n_new, -1)

    o_att, o_s, new_kt, new_vt = _attn_both(
        q, k_t, v_t, bias_prompt.reshape(N_DIL, N_PAIRS, HEADS_PER_PAIR * BLOCK, 2 * BLOCK),
        as_new(qs), as_new(ks), as_new(vs), _to_head_major(cache_win_k[0]), _to_head_major(cache_win_v[0]),
        bias_old.reshape(N_DIL, N_HEADS * n_new, wb), bias_new.reshape(N_DIL, N_HEADS * n_new, LANES))

    y_prompt, conv_p, lru_p = _lru_outproj(
        x_r, g_r, jnp.zeros((bp, CONV_W - 1, W_R), F32), jnp.zeros((bp, 1, W_R), state_lru.dtype), lru_args,
        o_att, g_a, x_prompt, row(norm_attn[0]), row(norm_final), w_out_b, tt=LRU_TILE)
    y_lru_s, conv_s, lru_s = _lru(as_new(xrs), as_new(grs), state_conv[0], state_lru[0].reshape(bs, 1, W_R),
                                  *lru_args, tt=n_new, nb=SAMPLE_LRU_BATCH)
    y_sample = _outproj(o_s.reshape(bs * n_new, W_A), gas, y_lru_s.reshape(bs * n_new, W_R), xs,
                        row(norm_attn[0]), row(norm_final), w_out_b, bs * n_new)

    return (y_prompt.reshape(bp, seq, D_MODEL), y_sample.reshape(bs, n_new, D_MODEL),
            _from_head_major(k_t), _from_head_major(v_t), conv_p[None], lru_p.reshape(1, bp, W_R),
            _from_head_major(new_kt), _from_head_major(new_vt), conv_s[None],
            lru_s.reshape(1, bs, W_R).astype(state_lru.dtype))
```

```python
import functools
import math

import numpy as np
import jax
import jax.numpy as jnp
from jax import lax
from jax.experimental import pallas as pl
from jax.experimental.pallas import tpu as pltpu

D_MODEL = 1024
HEAD_DIM = 64
N_HEADS = 16
W_A = N_HEADS * HEAD_DIM
W_R = D_MODEL
N_LRU_BLOCKS = 16
LRU_BLOCK = W_R // N_LRU_BLOCKS
N_PROJ = 6
CONV_W = 4
LRU_C = 8.0
DILATIONS = (1, 4, 16)
N_DIL = len(DILATIONS)
N_STEPS = 128
BLOCK = 128
N_BUCKETS = 32
MAX_EXACT = 16
MAX_DISTANCE = 2048
EPS = 1e-6
SCALE = HEAD_DIM ** -0.5
NEG_INF = -1e30

LANES = 128
SUBLANES = 8
HEADS_PER_PAIR = LANES // HEAD_DIM
N_PAIRS = N_HEADS // HEADS_PER_PAIR
MXU_TILE = 256
VMEM_LIMIT = 56 * 1024 * 1024

F32 = jnp.float32
BF16 = jnp.bfloat16
NT_DIMS = (((1,), (1,)), ((), ()))


def _params(n_axes, vmem_limit=VMEM_LIMIT):
    return pltpu.CompilerParams(dimension_semantics=("arbitrary",) * n_axes,
                                vmem_limit_bytes=vmem_limit)


def _rms_scale(x):
    return lax.rsqrt(jnp.mean(x * x, axis=-1, keepdims=True) + EPS)


def _sigmoid(x):
    return 0.5 * jnp.tanh(0.5 * x) + 0.5


def _silu(x):
    h = 0.5 * x
    return h * jnp.tanh(h) + h


def _resident(shape):
    return pl.BlockSpec(shape, lambda *_: (0,) * len(shape), pipeline_mode=pl.Buffered(1))


def _normed(x_ref, g_ref):
    x = x_ref[...]
    return ((x * _rms_scale(x)) * g_ref[...]).astype(BF16)


def _inproj_rows_body(x_ref, g_ref, w_ref, *out_refs):
    h = _normed(x_ref, g_ref)
    for i, o_ref in enumerate(out_refs):
        r = jnp.dot(h, w_ref[:, i * W_A:(i + 1) * W_A], preferred_element_type=F32)
        if i == 0:
            r = r * SCALE
        o_ref[...] = r


def _inproj_rows(x2d, norm_in, w_in_bf16):
    m = x2d.shape[0]
    row = pl.BlockSpec((m, D_MODEL), lambda i: (0, 0))
    return pl.pallas_call(
        _inproj_rows_body,
        grid=(1,),
        in_specs=[row, pl.BlockSpec((1, D_MODEL), lambda i: (0, 0)),
                  pl.BlockSpec((D_MODEL, N_PROJ * W_A), lambda i: (0, 0))],
        out_specs=[row] * N_PROJ,
        out_shape=[jax.ShapeDtypeStruct((m, W_A), F32)] * N_PROJ,
        compiler_params=_params(1),
        name="inproj_rows",
    )(x2d, norm_in, w_in_bf16)


def _inproj_prompt_body(x_ref, g_ref, wr_ref, wt_ref, q_ref, ga_ref, xr_ref, gr_ref, kr_ref, vr_ref, kt_ref, vt_ref):
    h = _normed(x_ref, g_ref)
    for i, (t_ref, r_ref) in enumerate(((kt_ref, kr_ref), (vt_ref, vr_ref))):
        t = lax.dot_general(wt_ref[i * W_A:(i + 1) * W_A, :], h, NT_DIMS, preferred_element_type=F32)
        t_ref[...] = t
        r_ref[...] = t.T
    for i, o_ref in ((0, q_ref), (3, ga_ref), (4, xr_ref), (5, gr_ref)):
        r = jnp.dot(h, wr_ref[:, i * W_A:(i + 1) * W_A], preferred_element_type=F32)
        if i == 0:
            r = r * SCALE
        o_ref[...] = r


def _inproj_prompt(x, norm_in, w_all, w_kv_t, tm):
    b, seq, _ = x.shape
    row = pl.BlockSpec((None, tm, D_MODEL), lambda i, j: (i, j, 0))
    col = pl.BlockSpec((None, W_A, tm), lambda i, j: (i, 0, j))
    rows_shape = jax.ShapeDtypeStruct((b, seq, W_A), F32)
    cols_shape = jax.ShapeDtypeStruct((b, W_A, seq), F32)
    return pl.pallas_call(
        _inproj_prompt_body,
        grid=(b, seq // tm),
        in_specs=[row, _resident((1, D_MODEL)), _resident(w_all.shape), _resident(w_kv_t.shape)],
        out_specs=[row] * 6 + [col] * 2,
        out_shape=[rows_shape] * 6 + [cols_shape] * 2,
        compiler_params=_params(2),
        name="inproj_prompt",
    )(x, norm_in, w_all, w_kv_t)


def _t5_bucket_np(dist):
    nf = np.maximum(dist, 1).astype(np.float32)
    large = MAX_EXACT + (np.log(nf / np.float32(MAX_EXACT)) / np.float32(math.log(MAX_DISTANCE / MAX_EXACT))
                         * np.float32(N_BUCKETS - MAX_EXACT)).astype(np.int32)
    large = np.minimum(large, N_BUCKETS - 1)
    return np.where(dist < MAX_EXACT, dist, large).astype(np.int32)


def _bucket_index_tables(n_new, wb):
    i = np.arange(BLOCK)[:, None]
    j = np.arange(2 * BLOCK)[None, :]
    diff = i - j + BLOCK
    band = (diff >= 0) & (diff <= N_STEPS)
    t = np.arange(n_new)[:, None]
    dist_old = wb + t - np.arange(wb)[None, :]
    dist_new = t - np.arange(LANES)[None, :]
    prompt, s_old, s_new = [], [], []
    for dil in DILATIONS:
        prompt.append(np.where(band, _t5_bucket_np(np.maximum(diff, 0) * dil), -1))
        for dist, out, extra in ((dist_old, s_old, True), (dist_new, s_new, np.arange(LANES)[None, :] < n_new)):
            ok = (dist >= 0) & (dist % dil == 0) & (dist // dil <= N_STEPS) & extra
            out.append(np.where(ok, _t5_bucket_np(np.maximum(dist, 0)), -1))
    as_i32 = lambda a: np.stack(a).astype(np.int32)
    return as_i32(prompt), as_i32(s_old), as_i32(s_new)


def _bias_tables_body(rb_ref, *refs, buckets):
    idx_refs, out_refs = refs[:3], refs[3:]
    for d in range(N_DIL):
        idx = [r[d] for r in idx_refs]

        def per_head(h, carry):
            tabs = [jnp.full(ix.shape, NEG_INF, F32) for ix in idx]
            for b in buckets[d]:
                val = rb_ref[b, h]
                tabs = [jnp.where(ix == b, val, tb) for ix, tb in zip(idx, tabs)]
            for o_ref, tb in zip(out_refs, tabs):
                o_ref[d, h] = tb
            return carry
        lax.fori_loop(0, N_HEADS, per_head, 0)


def _bias_tables(rel_bias, n_new, wb):
    idx = _bucket_index_tables(n_new, wb)
    buckets = tuple(tuple(int(b) for b in np.unique(np.concatenate([a[d].ravel() for a in idx])) if b >= 0)
                    for d in range(N_DIL))
    vmem = pl.BlockSpec(memory_space=pltpu.VMEM)
    return pl.pallas_call(
        functools.partial(_bias_tables_body, buckets=buckets),
        in_specs=[pl.BlockSpec(memory_space=pltpu.SMEM)] + [vmem] * 3,
        out_specs=[vmem] * 3,
        out_shape=[jax.ShapeDtypeStruct((N_DIL, N_HEADS) + a.shape[1:], F32) for a in idx],
        compiler_params=pltpu.CompilerParams(vmem_limit_bytes=VMEM_LIMIT),
        name="bias_tables",
    )(rel_bias, *[jnp.asarray(a) for a in idx])


INTERLEAVE = 4


def _prompt_scratch(seq):
    n_blocks = seq // BLOCK
    return ([pltpu.VMEM((seq, LANES), F32)] * 9
            + [pltpu.VMEM((n_blocks, HEADS_PER_PAIR * BLOCK, 2 * BLOCK), BF16)])


def _attn_prompt_body(q_ref, k_ref, v_ref, bm_ref, o_ref, qg_ref, kg_ref, vg_ref,
                      acc16_ref, m16_ref, l16_ref, accp_ref, mp_ref, lp_ref, e_ref, *, seq, extra_work=None):
    group = seq // INTERLEAVE
    nb4 = group // BLOCK
    n_blocks = seq // BLOCK
    assert DILATIONS == (1, INTERLEAVE, INTERLEAVE * INTERLEAVE) and seq == DILATIONS[2] * BLOCK

    first = lax.broadcasted_iota(jnp.int32, (BLOCK, LANES), 1) < HEAD_DIM
    pick = lambda a: jnp.where(first, a[:BLOCK], a[BLOCK:])
    grouped = (qg_ref, kg_ref, vg_ref)
    natural = (q_ref, k_ref, v_ref)

    one_trip = jnp.minimum(pl.program_id(0) + 1, 1)

    def region(fn):
        def body(_, carry):
            fn()
            return carry
        lax.fori_loop(0, one_trip, body, 0)

    def rows_of(start, n, stride):
        return pl.ds(start, n) if stride == 1 else pl.ds(start, n, stride=stride)

    def scores(slot, dil_idx, refs, stride, q_start, k_start, with_prev):
        qs_ref, ks_ref, _ = refs
        q = qs_ref[rows_of(q_start, BLOCK, stride), :]
        q2 = jnp.concatenate([jnp.where(first, q, 0.0), jnp.where(first, 0.0, q)], axis=0).astype(BF16)
        nk = 2 * BLOCK if with_prev else BLOCK
        kb = ks_ref[rows_of(k_start, nk, stride), :].astype(BF16)
        bias = bm_ref[dil_idx] if with_prev else bm_ref[dil_idx, :, BLOCK:]
        s = lax.dot_general(q2, kb, NT_DIMS, preferred_element_type=F32) + bias
        m = jnp.max(s, axis=-1, keepdims=True)
        e_ref[slot, :, 0:nk] = jnp.exp(s - m).astype(BF16)
        return pick(m)

    def values(slot, refs, stride, k_start, with_prev):
        nk = 2 * BLOCK if with_prev else BLOCK
        vb = refs[2][rows_of(k_start, nk, stride), :].astype(BF16)
        v1 = jnp.concatenate([vb, jnp.ones((nk, LANES), BF16)], axis=1)
        acc = jnp.dot(e_ref[slot, :, 0:nk], v1, preferred_element_type=F32)
        return pick(acc[:, :LANES]), pick(acc[:, LANES:])

    def merge(parts):
        mx = functools.reduce(jnp.maximum, [m for _, m, _ in parts])
        scale = [jnp.exp(m - mx) for _, m, _ in parts]
        acc = functools.reduce(lambda x, y: x + y, [w * a for w, (a, _, _) in zip(scale, parts)])
        den = functools.reduce(lambda x, y: x + y, [w * l for w, (_, _, l) in zip(scale, parts)])
        return acc, mx, den

    d16_start = [(idx // INTERLEAVE) * group + idx % INTERLEAVE for idx in range(n_blocks)]
    d16_rows = [pl.ds(st, BLOCK, stride=INTERLEAVE) for st in d16_start]
    d4_blocks = [(g * group + n * BLOCK, n > 0) for g in range(INTERLEAVE) for n in range(nb4)]
    d4_rows = [pl.ds(g + n * BLOCK * INTERLEAVE, BLOCK, stride=INTERLEAVE) for g in range(INTERLEAVE) for n in range(nb4)]

    def setup_and_d16_scores():
        for src, dst in zip(natural, grouped):
            for g in range(INTERLEAVE):
                dst[g * group:(g + 1) * group, :] = src[pl.ds(g, group, stride=INTERLEAVE), :]
        for slot, st in enumerate(d16_start):
            m16_ref[d16_rows[slot], :] = scores(slot, 2, grouped, INTERLEAVE, st, st, False)

    def d16_values_and_d4_scores():
        for slot, st in enumerate(d16_start):
            acc, l = values(slot, grouped, INTERLEAVE, st, False)
            acc16_ref[d16_rows[slot], :] = acc
            l16_ref[d16_rows[slot], :] = l
        for slot, (st, with_prev) in enumerate(d4_blocks):
            accp_ref[d4_rows[slot], :] = scores(slot, 1, grouped, 1, st, st - BLOCK if with_prev else st, with_prev)

    def d4_values_and_d1_scores():
        for slot, (st, with_prev) in enumerate(d4_blocks):
            acc, l = values(slot, grouped, 1, st - BLOCK if with_prev else st, with_prev)
            rows = pl.ds(st, BLOCK)
            vals = merge([(acc, accp_ref[d4_rows[slot], :], l),
                          (acc16_ref[rows, :], m16_ref[rows, :], l16_ref[rows, :])])
            for ref, val in zip((accp_ref, mp_ref, lp_ref), vals):
                ref[d4_rows[slot], :] = val
        for n in range(n_blocks):
            o_ref[pl.ds(n * BLOCK, BLOCK), :] = scores(n, 0, natural, 1, n * BLOCK, max(n - 1, 0) * BLOCK, n > 0)

    def d1_values():
        for n in range(n_blocks):
            rows = pl.ds(n * BLOCK, BLOCK)
            acc, l = values(n, natural, 1, max(n - 1, 0) * BLOCK, n > 0)
            acc, _, den = merge([(acc, o_ref[rows, :], l), (accp_ref[rows, :], mp_ref[rows, :], lp_ref[rows, :])])
            o_ref[rows, :] = acc / den

    stages = (setup_and_d16_scores, d16_values_and_d4_scores, d4_values_and_d1_scores, d1_values)
    for stage, extra in zip(stages, extra_work or (None,) * len(stages)):
        region(stage if extra is None else lambda stage=stage, extra=extra: (extra(), stage()))


def _attn_prompt(q, k, v, bias_tabs):
    b, seq, _ = q.shape
    rows = pl.BlockSpec((None, seq, LANES), lambda p, i: (i, 0, p))
    scratch = _prompt_scratch(seq)
    return pl.pallas_call(
        functools.partial(_attn_prompt_body, seq=seq),
        grid=(N_PAIRS, b),
        in_specs=[rows, rows, rows,
                  pl.BlockSpec((N_DIL, None, HEADS_PER_PAIR * BLOCK, 2 * BLOCK), lambda p, i: (0, p, 0, 0))],
        out_specs=rows,
        out_shape=jax.ShapeDtypeStruct((b, seq, W_A), F32),
        scratch_shapes=scratch,
        compiler_params=_params(2),
        name="attn_prompt",
    )(q, k, v, bias_tabs)


SAMPLE_HEADS = 8
SAMPLE_LANES = SAMPLE_HEADS * HEAD_DIM


def _padded_new(new_ref, n_new):
    return jnp.concatenate([new_ref[...], jnp.zeros((LANES - n_new, SAMPLE_LANES), F32)], axis=0)


def _window_update(c_ref, new_ref, out_ref, *, wb, n_new, rows=(0, SAMPLE_LANES)):
    lo, hi = rows
    last = lax.broadcasted_iota(jnp.int32, (hi - lo, LANES), 1) >= LANES - n_new
    rolled = pltpu.roll(c_ref[lo:hi, :], wb - n_new, axis=1)
    out_ref[lo:hi, :] = rolled
    new_t = _padded_new(new_ref, n_new).T[lo:hi, :]
    tail = pltpu.roll(new_t, LANES - n_new, axis=1)
    out_ref[lo:hi, wb - LANES:wb] = jnp.where(last, tail, rolled[:, wb - LANES:wb])


def _attn_sample_body(q_ref, kn_ref, vn_ref, ckt_ref, cvt_ref, bo_ref, bn_ref, o_ref, nkt_ref, nvt_ref,
                      *, wb, n_new):
    _sample_attention(q_ref, kn_ref, vn_ref, ckt_ref, cvt_ref, bo_ref, bn_ref, o_ref, n_new=n_new)
    _window_update(ckt_ref, kn_ref, nkt_ref, wb=wb, n_new=n_new)
    _window_update(cvt_ref, vn_ref, nvt_ref, wb=wb, n_new=n_new)


def _sample_attention(q_ref, kn_ref, vn_ref, ckt_ref, cvt_ref, bo_ref, bn_ref, o_ref, *, n_new):
    rows = SAMPLE_HEADS * n_new
    row_head = lax.broadcasted_iota(jnp.int32, (rows, SAMPLE_LANES), 0) // n_new
    col_head = lax.broadcasted_iota(jnp.int32, (rows, SAMPLE_LANES), 1) // HEAD_DIM
    own_head = row_head == col_head
    q_rep = jnp.concatenate([q_ref[...]] * SAMPLE_HEADS, axis=0)
    q_bd = jnp.where(own_head, q_rep, 0.0).astype(BF16)

    kn = _padded_new(kn_ref, n_new)
    vn = _padded_new(vn_ref, n_new)
    kt = ckt_ref[...]
    vt = cvt_ref[...]

    s_old = jnp.dot(q_bd, kt.astype(BF16), preferred_element_type=F32)
    s_new = lax.dot_general(q_bd, kn.astype(BF16), NT_DIMS, preferred_element_type=F32)
    lo = [s_old + bo_ref[d] for d in range(N_DIL)]
    ln = [s_new + bn_ref[d] for d in range(N_DIL)]
    row_max = lambda t: jnp.max(t, axis=-1, keepdims=True)
    mx = functools.reduce(jnp.maximum, [row_max(t) for t in lo + ln])
    e_old = functools.reduce(lambda a, b: a + b, [jnp.exp(t - mx) for t in lo])
    e_new = functools.reduce(lambda a, b: a + b, [jnp.exp(t - mx) for t in ln])
    den = jnp.sum(e_old, axis=-1, keepdims=True) + jnp.sum(e_new, axis=-1, keepdims=True)
    o_full = lax.dot_general(e_old.astype(BF16), vt.astype(BF16), NT_DIMS, preferred_element_type=F32)
    o_full = o_full + jnp.dot(e_new.astype(BF16), vn.astype(BF16), preferred_element_type=F32)
    o_full = jnp.where(own_head, o_full / den, 0.0)
    out = o_full[0:n_new, :]
    for h in range(1, SAMPLE_HEADS):
        out = out + o_full[h * n_new:(h + 1) * n_new, :]
    o_ref[...] = out


def _attn_sample(q, k_new, v_new, cache_kt, cache_vt, bias_old, bias_new):
    b, n_new, _ = q.shape
    wb = cache_kt.shape[2]
    n_groups = N_HEADS // SAMPLE_HEADS
    new_spec = pl.BlockSpec((None, n_new, SAMPLE_LANES), lambda i, g: (i, 0, g))
    cache_spec = pl.BlockSpec((None, SAMPLE_LANES, wb), lambda i, g: (i, g, 0))
    rows = SAMPLE_HEADS * n_new
    return pl.pallas_call(
        functools.partial(_attn_sample_body, wb=wb, n_new=n_new),
        grid=(b, n_groups),
        in_specs=[new_spec, new_spec, new_spec, cache_spec, cache_spec,
                  pl.BlockSpec((N_DIL, rows, wb), lambda i, g: (0, g, 0)),
                  pl.BlockSpec((N_DIL, rows, LANES), lambda i, g: (0, g, 0))],
        out_specs=[new_spec, cache_spec, cache_spec],
        out_shape=[jax.ShapeDtypeStruct((b, n_new, W_A), F32),
                   jax.ShapeDtypeStruct((b, W_A, wb), F32),
                   jax.ShapeDtypeStruct((b, W_A, wb), F32)],
        compiler_params=_params(2),
        name="attn_sample",
    )(q, k_new, v_new, cache_kt, cache_vt, bias_old, bias_new)


N_PROMPT_IN, N_SAMPLE_IN, N_SAMPLE_OUT = 4, 7, 3
VMEM_LIMIT_BOTH = 62 * 1024 * 1024


def _attn_both_body(*refs, seq, wb, n_new):
    n_in = N_PROMPT_IN + N_SAMPLE_IN
    prompt_in, sample_in = refs[:N_PROMPT_IN], refs[N_PROMPT_IN:n_in]
    o_ref, sample_out, scratch = refs[n_in], refs[n_in + 1:n_in + 1 + N_SAMPLE_OUT], refs[n_in + 1 + N_SAMPLE_OUT:]
    _, kn_ref, vn_ref, ckt_ref, cvt_ref, _, _ = sample_in
    os_ref, nkt_ref, nvt_ref = sample_out
    half = SAMPLE_LANES // 2
    update_k = functools.partial(_window_update, ckt_ref, kn_ref, nkt_ref, wb=wb, n_new=n_new)
    update_v = functools.partial(_window_update, cvt_ref, vn_ref, nvt_ref, wb=wb, n_new=n_new)
    attend = functools.partial(_sample_attention, *sample_in, os_ref, n_new=n_new)
    extra_work = (None, functools.partial(update_k, rows=(0, half)), functools.partial(update_v, rows=(0, half)),
                  lambda: (update_k(rows=(half, SAMPLE_LANES)), update_v(rows=(half, SAMPLE_LANES)), attend()))
    _attn_prompt_body(*prompt_in, o_ref, *scratch, seq=seq, extra_work=extra_work)


def _attn_both(q, k, v, bias_prompt, qs, k_new, v_new, cache_kt, cache_vt, bias_old, bias_new):
    bp, seq, _ = q.shape
    bs, n_new, _ = qs.shape
    wb = cache_kt.shape[2]
    n_groups = N_HEADS // SAMPLE_HEADS
    assert N_PAIRS * bp == bs * n_groups, "one sample (batch, head group) per prompt (head pair, batch) step"
    step = lambda p, i: p * bp + i
    rows = pl.BlockSpec((None, seq, LANES), lambda p, i: (i, 0, p))
    new_spec = pl.BlockSpec((None, n_new, SAMPLE_LANES), lambda p, i: (step(p, i) // n_groups, 0, step(p, i) % n_groups))
    cache_spec = pl.BlockSpec((None, SAMPLE_LANES, wb), lambda p, i: (step(p, i) // n_groups, step(p, i) % n_groups, 0))
    s_rows = SAMPLE_HEADS * n_new
    scratch = _prompt_scratch(seq)
    return pl.pallas_call(
        functools.partial(_attn_both_body, seq=seq, wb=wb, n_new=n_new),
        grid=(N_PAIRS, bp),
        in_specs=[rows, rows, rows,
                  pl.BlockSpec((N_DIL, None, HEADS_PER_PAIR * BLOCK, 2 * BLOCK), lambda p, i: (0, p, 0, 0)),
                  new_spec, new_spec, new_spec, cache_spec, cache_spec,
                  pl.BlockSpec((N_DIL, s_rows, wb), lambda p, i: (0, step(p, i) % n_groups, 0)),
                  pl.BlockSpec((N_DIL, s_rows, LANES), lambda p, i: (0, step(p, i) % n_groups, 0))],
        out_specs=[rows, new_spec, cache_spec, cache_spec],
        out_shape=[jax.ShapeDtypeStruct((bp, seq, W_A), F32),
                   jax.ShapeDtypeStruct((bs, n_new, W_A), F32),
                   jax.ShapeDtypeStruct((bs, W_A, wb), F32),
                   jax.ShapeDtypeStruct((bs, W_A, wb), F32)],
        scratch_shapes=scratch,
        compiler_params=_params(2, VMEM_LIMIT_BOTH),
        name="attn_both",
    )(q, k, v, bias_prompt, qs, k_new, v_new, cache_kt, cache_vt, bias_old, bias_new)


def _lru_core(xr_ref, gr_ref, cpast_ref, hpast_ref, cw_ref, cb_ref, wg_ref, bgx_ref, bga_ref, lp_ref, nl_ref,
              nconv_ref, nlru_ref, tail_ref, h_ref, hs_ref, tt):
    t = pl.program_id(1)
    n_t = pl.num_programs(1)

    @pl.when(t == 0)
    def _():
        tail_ref[...] = jnp.concatenate(
            [jnp.zeros((SUBLANES - (CONV_W - 1), W_R), F32), cpast_ref[...]], axis=0)
        h_ref[...] = hpast_ref[...]

    x = xr_ref[...]
    xe = jnp.concatenate([tail_ref[...], x], axis=0)
    xc = cb_ref[...] + cw_ref[CONV_W - 1:CONV_W, :] * x
    for tap in range(CONV_W - 1):
        back = CONV_W - 1 - tap
        xc = xc + cw_ref[tap:tap + 1, :] * xe[SUBLANES - back:SUBLANES - back + tt, :]
    tail_ref[...] = x[tt - SUBLANES:tt, :]

    xcb = xc.astype(BF16)
    gx, ga = [], []
    for j in range(W_R // MXU_TILE):
        g = jnp.dot(xcb[:, j * MXU_TILE:(j + 1) * MXU_TILE], wg_ref[j], preferred_element_type=F32)
        gx.append(g[:, :MXU_TILE])
        ga.append(g[:, MXU_TILE:])
    gate_x = _sigmoid(jnp.concatenate(gx, axis=-1) + bgx_ref[...])
    gate_a = _sigmoid(jnp.concatenate(ga, axis=-1) + bga_ref[...])
    log_a = gate_a * ((-LRU_C) * jax.nn.softplus(-lp_ref[...]))
    a = jnp.exp(log_a)
    var = jnp.tanh(-log_a) * (1.0 + a * a)
    root = jnp.where(var > 0.0, var * lax.rsqrt(var), 0.0)
    bx = root * (gate_x * xc)

    groups = tt // SUBLANES
    a3 = a.reshape(groups, SUBLANES, W_R)
    b3 = bx.reshape(groups, SUBLANES, W_R)
    sub = lax.broadcasted_iota(jnp.int32, (groups, SUBLANES, W_R), 1)
    shift = 1
    while shift < SUBLANES:
        keep = sub >= shift
        b_prev = jnp.where(keep, pltpu.roll(b3, shift, axis=1), 0.0)
        a_prev = jnp.where(keep, pltpu.roll(a3, shift, axis=1), 1.0)
        b3 = b3 + a3 * b_prev
        a3 = a3 * a_prev
        shift *= 2
    h = h_ref[...]
    for g in range(groups):
        hg = a3[g] * h + b3[g]
        hs_ref[g * SUBLANES:(g + 1) * SUBLANES, :] = hg
        h = hg[SUBLANES - 1:SUBLANES, :]
    h_ref[...] = h

    @pl.when(t == n_t - 1)
    def _():
        nconv_ref[...] = x[tt - (CONV_W - 1):tt, :]
        nlru_ref[...] = h

    o = hs_ref[...]
    return (((o * _rms_scale(o)) * nl_ref[...]) * _silu(gr_ref[...])).astype(BF16)


N_LRU_IN = 11


N_LRU_PER_BATCH = 4


def _lru_body(*refs, tt, nb):
    ins, (y_ref, nconv_ref, nlru_ref), scratch = refs[:N_LRU_IN], refs[N_LRU_IN:N_LRU_IN + 3], refs[N_LRU_IN + 3:]
    if nb is None:
        y_ref[...] = _lru_core(*ins, nconv_ref, nlru_ref, *scratch, tt)
        return
    for bi in range(nb):
        ins_b = [r.at[bi] if k < N_LRU_PER_BATCH else r for k, r in enumerate(ins)]
        y_ref[bi] = _lru_core(*ins_b, nconv_ref.at[bi], nlru_ref.at[bi], *scratch, tt)


def _lru_specs(b, t_len, tt, nb=None):
    assert t_len % tt == 0 and tt % SUBLANES == 0 and tt >= SUBLANES
    assert nb is None or (t_len == tt and b % nb == 0)
    seq_spec = pl.BlockSpec((nb, tt, W_R), lambda i, j: (i, j, 0))
    vec = pl.BlockSpec((1, W_R), lambda i, j: (0, 0))
    conv_spec = pl.BlockSpec((nb, CONV_W - 1, W_R), lambda i, j: (i, 0, 0))
    h_spec = pl.BlockSpec((nb, 1, W_R), lambda i, j: (i, 0, 0))
    in_specs = [seq_spec, seq_spec, conv_spec, h_spec,
                pl.BlockSpec((CONV_W, W_R), lambda i, j: (0, 0)), vec,
                pl.BlockSpec((W_R // MXU_TILE, MXU_TILE, 2 * MXU_TILE), lambda i, j: (0, 0, 0)),
                vec, vec, vec, vec]
    state_shapes = [jax.ShapeDtypeStruct((b, CONV_W - 1, W_R), F32), jax.ShapeDtypeStruct((b, 1, W_R), F32)]
    scratch = [pltpu.VMEM((SUBLANES, W_R), F32), pltpu.VMEM((1, W_R), F32), pltpu.VMEM((tt, W_R), F32)]
    return seq_spec, vec, in_specs, [conv_spec, h_spec], state_shapes, scratch


def _lru(x_r, g_r, conv_past, h_past, conv_w, conv_b, w_gates, b_gate_x, b_gate_a, lru_param, norm_lru, tt, nb=None):
    b, t_len, _ = x_r.shape
    seq_spec, _, in_specs, state_specs, state_shapes, scratch = _lru_specs(b, t_len, tt, nb)
    return pl.pallas_call(
        functools.partial(_lru_body, tt=tt, nb=nb),
        grid=(b // (nb or 1), t_len // tt),
        in_specs=in_specs,
        out_specs=[seq_spec] + state_specs,
        out_shape=[jax.ShapeDtypeStruct((b, t_len, W_R), BF16)] + state_shapes,
        scratch_shapes=scratch,
        compiler_params=_params(2),
        name="rglru",
    )(x_r, g_r, conv_past, h_past, conv_w, conv_b, w_gates, b_gate_x, b_gate_a, lru_param, norm_lru)


def _gate_weight_tiles(w_gate_x, w_gate_a):
    per_tile = MXU_TILE // LRU_BLOCK
    eye = jnp.eye(per_tile, dtype=F32)

    def tiles(w):
        w = w.reshape(N_LRU_BLOCKS // per_tile, per_tile, LRU_BLOCK, LRU_BLOCK)
        bd = jnp.einsum('jaik,ab->jaibk', w, eye)
        return bd.reshape(N_LRU_BLOCKS // per_tile, MXU_TILE, MXU_TILE)
    return jnp.concatenate([tiles(w_gate_x), tiles(w_gate_a)], axis=-1).astype(BF16)


def _outproj_core(o_ref, ga_ref, y_lru, x_ref, na_ref, nf_ref, w_ref):
    o = o_ref[...]
    ya = ((o * _rms_scale(o)) * na_ref[...]) * _silu(ga_ref[...])
    acc = jnp.dot(ya.astype(BF16), w_ref[0:W_A, :], preferred_element_type=F32)
    acc = acc + jnp.dot(y_lru, w_ref[W_A:W_A + W_R, :], preferred_element_type=F32)
    y = x_ref[...] + acc
    return (y * _rms_scale(y)) * nf_ref[...]


def _outproj_body(o_ref, ga_ref, yl_ref, x_ref, na_ref, nf_ref, w_ref, y_ref):
    y_ref[...] = _outproj_core(o_ref, ga_ref, yl_ref[...], x_ref, na_ref, nf_ref, w_ref)


N_OUT_IN = 6


def _lru_outproj_body(*refs, tt):
    lru_in, out_in = refs[:N_LRU_IN], refs[N_LRU_IN:N_LRU_IN + N_OUT_IN]
    y_ref, nconv_ref, nlru_ref = refs[N_LRU_IN + N_OUT_IN:N_LRU_IN + N_OUT_IN + 3]
    scratch = refs[N_LRU_IN + N_OUT_IN + 3:]
    o_ref, ga_ref, x_ref, na_ref, nf_ref, w_ref = out_in
    y_lru = _lru_core(*lru_in, nconv_ref, nlru_ref, *scratch, tt)
    y_ref[...] = _outproj_core(o_ref, ga_ref, y_lru, x_ref, na_ref, nf_ref, w_ref)


def _lru_outproj(x_r, g_r, conv_past, h_past, lru_params, o_att, g_a, x, norm_attn, norm_final, w_out_bf16, tt):
    b, t_len, _ = x_r.shape
    seq_spec, vec, in_specs, state_specs, state_shapes, scratch = _lru_specs(b, t_len, tt)
    return pl.pallas_call(
        functools.partial(_lru_outproj_body, tt=tt),
        grid=(b, t_len // tt),
        in_specs=in_specs + [seq_spec, seq_spec, seq_spec, vec, vec, _resident(w_out_bf16.shape)],
        out_specs=[seq_spec] + state_specs,
        out_shape=[jax.ShapeDtypeStruct((b, t_len, D_MODEL), F32)] + state_shapes,
        scratch_shapes=scratch,
        compiler_params=_params(2),
        name="rglru_outproj",
    )(x_r, g_r, conv_past, h_past, *lru_params, o_att, g_a, x, norm_attn, norm_final, w_out_bf16)


def _outproj(o_att, g_a, y_lru, x2d, norm_attn, norm_final, w_out_bf16, tm):
    m = x2d.shape[0]
    row = pl.BlockSpec((tm, D_MODEL), lambda i: (i, 0))
    vec = pl.BlockSpec((1, D_MODEL), lambda i: (0, 0))
    return pl.pallas_call(
        _outproj_body,
        grid=(m // tm,),
        in_specs=[row, row, row, row, vec, vec, pl.BlockSpec((W_A + W_R, D_MODEL), lambda i: (0, 0))],
        out_specs=row,
        out_shape=jax.ShapeDtypeStruct((m, D_MODEL), F32),
        compiler_params=_params(1),
        name="outproj",
    )(o_att, g_a, y_lru, x2d, norm_attn, norm_final, w_out_bf16)


PROMPT_ROWS = 512
LRU_TILE = 512
SAMPLE_LRU_BATCH = 8


def _to_head_major(win):
    b, p = win.shape[0], win.shape[1]
    return jnp.transpose(win, (0, 2, 3, 1)).reshape(b, W_A, p)


def _from_head_major(win_t):
    b, _, p = win_t.shape
    return jnp.transpose(win_t.reshape(b, N_HEADS, HEAD_DIM, p), (0, 3, 1, 2))[None]


def kernel(x_prompt, x_sample, cache_win_k, cache_win_v, state_conv, state_lru, rel_bias, norm_in, w_in, norm_attn, norm_lru, conv_w, conv_b, w_gate_x, b_gate_x, w_gate_a, b_gate_a, lru_param, w_out, norm_final):
    depth = w_in.shape[0]
    assert depth == 1, "single-layer trunk"
    bp, seq, _ = x_prompt.shape
    bs, n_new, _ = x_sample.shape
    wb = cache_win_k.shape[2]
    row = lambda p: p.reshape(1, -1)

    w_in_b = w_in[0].astype(BF16)
    w_kv_t = w_in_b[:, W_A:3 * W_A].T
    w_out_b = w_out[0].astype(BF16)
    w_gates = _gate_weight_tiles(w_gate_x[0], w_gate_a[0])
    lru_args = (conv_w[0], row(conv_b[0]), w_gates, row(b_gate_x[0]), row(b_gate_a[0]), row(lru_param[0]), row(norm_lru[0]))
    bias_prompt, bias_old, bias_new = _bias_tables(rel_bias, n_new, wb)

    q, g_a, x_r, g_r, k, v, k_t, v_t = _inproj_prompt(x_prompt, row(norm_in[0]), w_in_b, w_kv_t, PROMPT_ROWS)
    xs = x_sample.reshape(bs * n_new, D_MODEL)
    qs, ks, vs, gas, xrs, grs = _inproj_rows(xs, row(norm_in[0]), w_in_b)
    as_new = lambda a: a.reshape(bs, n_new, -1)

    o_att, o_s, new_kt, new_vt = _attn_both(
        q, k, v, bias_prompt.reshape(N_DIL, N_PAIRS, HEADS_PER_PAIR * BLOCK, 2 * BLOCK),
        as_new(qs), as_new(ks), as_new(vs), _to_head_major(cache_win_k[0]), _to_head_major(cache_win_v[0]),
        bias_old.reshape(N_DIL, N_HEADS * n_new, wb), bias_new.reshape(N_DIL, N_HEADS * n_new, LANES))

    y_prompt, conv_p, lru_p = _lru_outproj(
        x_r, g_r, jnp.zeros((bp, CONV_W - 1, W_R), F32), jnp.zeros((bp, 1, W_R), state_lru.dtype), lru_args,
        o_att, g_a, x_prompt, row(norm_attn[0]), row(norm_final), w_out_b, tt=LRU_TILE)
    y_lru_s, conv_s, lru_s = _lru(as_new(xrs), as_new(grs), state_conv[0], state_lru[0].reshape(bs, 1, W_R),
                                  *lru_args, tt=n_new, nb=SAMPLE_LRU_BATCH)
    y_sample = _outproj(o_s.reshape(bs * n_new, W_A), gas, y_lru_s.reshape(bs * n_new, W_R), xs,
                        row(norm_attn[0]), row(norm_final), w_out_b, bs * n_new)

    return (y_prompt.reshape(bp, seq, D_MODEL), y_sample.reshape(bs, n_new, D_MODEL),
            _from_head_major(k_t), _from_head_major(v_t), conv_p[None], lru_p.reshape(1, bp, W_R),
            _from_head_major(new_kt), _from_head_major(new_vt), conv_s[None],
            lru_s.reshape(1, bs, W_R).astype(state_lru.dtype))
```

```python
import functools
import math

import numpy as np
import jax
import jax.numpy as jnp
from jax import lax
from jax.experimental import pallas as pl
from jax.experimental.pallas import tpu as pltpu

D_MODEL = 1024
HEAD_DIM = 64
N_HEADS = 16
W_A = N_HEADS * HEAD_DIM
W_R = D_MODEL
N_LRU_BLOCKS = 16
LRU_BLOCK = W_R // N_LRU_BLOCKS
N_PROJ = 6
CONV_W = 4
LRU_C = 8.0
DILATIONS = (1, 4, 16)
N_DIL = len(DILATIONS)
N_STEPS = 128
BLOCK = 128
N_BUCKETS = 32
MAX_EXACT = 16
MAX_DISTANCE = 2048
EPS = 1e-6
SCALE = HEAD_DIM ** -0.5
NEG_INF = -1e30

LANES = 128
SUBLANES = 8
HEADS_PER_PAIR = LANES // HEAD_DIM
N_PAIRS = N_HEADS // HEADS_PER_PAIR
MXU_TILE = 256
VMEM_LIMIT = 56 * 1024 * 1024

F32 = jnp.float32
BF16 = jnp.bfloat16
NT_DIMS = (((1,), (1,)), ((), ()))


def _params(n_axes, vmem_limit=VMEM_LIMIT):
    return pltpu.CompilerParams(dimension_semantics=("arbitrary",) * n_axes,
                                vmem_limit_bytes=vmem_limit)


def _rms_scale(x):
    return lax.rsqrt(jnp.mean(x * x, axis=-1, keepdims=True) + EPS)


def _sigmoid(x):
    return 0.5 * jnp.tanh(0.5 * x) + 0.5


def _silu(x):
    h = 0.5 * x
    return h * jnp.tanh(h) + h


def _resident(shape):
    return pl.BlockSpec(shape, lambda *_: (0,) * len(shape), pipeline_mode=pl.Buffered(1))


def _normed(x_ref, g_ref):
    x = x_ref[...]
    return ((x * _rms_scale(x)) * g_ref[...]).astype(BF16)


def _inproj_rows_body(x_ref, g_ref, w_ref, *out_refs):
    h = _normed(x_ref, g_ref)
    for i, o_ref in enumerate(out_refs):
        r = jnp.dot(h, w_ref[:, i * W_A:(i + 1) * W_A], preferred_element_type=F32)
        if i == 0:
            r = r * SCALE
        o_ref[...] = r


def _inproj_rows(x2d, norm_in, w_in_bf16):
    m = x2d.shape[0]
    row = pl.BlockSpec((m, D_MODEL), lambda i: (0, 0))
    return pl.pallas_call(
        _inproj_rows_body,
        grid=(1,),
        in_specs=[row, pl.BlockSpec((1, D_MODEL), lambda i: (0, 0)),
                  pl.BlockSpec((D_MODEL, N_PROJ * W_A), lambda i: (0, 0))],
        out_specs=[row] * N_PROJ,
        out_shape=[jax.ShapeDtypeStruct((m, W_A), F32)] * N_PROJ,
        compiler_params=_params(1),
        name="inproj_rows",
    )(x2d, norm_in, w_in_bf16)


def _inproj_prompt_body(x_ref, g_ref, wr_ref, wt_ref, q_ref, ga_ref, xr_ref, gr_ref, kr_ref, vr_ref, kt_ref, vt_ref):
    h = _normed(x_ref, g_ref)
    for i, (t_ref, r_ref) in enumerate(((kt_ref, kr_ref), (vt_ref, vr_ref))):
        t = lax.dot_general(wt_ref[i * W_A:(i + 1) * W_A, :], h, NT_DIMS, preferred_element_type=F32)
        t_ref[...] = t
        r_ref[...] = t.T
    for i, o_ref in ((0, q_ref), (3, ga_ref), (4, xr_ref), (5, gr_ref)):
        r = jnp.dot(h, wr_ref[:, i * W_A:(i + 1) * W_A], preferred_element_type=F32)
        if i == 0:
            r = r * SCALE
        o_ref[...] = r


def _inproj_prompt(x, norm_in, w_all, w_kv_t, tm):
    b, seq, _ = x.shape
    row = pl.BlockSpec((None, tm, D_MODEL), lambda i, j: (i, j, 0))
    col = pl.BlockSpec((None, W_A, tm), lambda i, j: (i, 0, j))
    rows_shape = jax.ShapeDtypeStruct((b, seq, W_A), F32)
    cols_shape = jax.ShapeDtypeStruct((b, W_A, seq), F32)
    return pl.pallas_call(
        _inproj_prompt_body,
        grid=(b, seq // tm),
        in_specs=[row, _resident((1, D_MODEL)), _resident(w_all.shape), _resident(w_kv_t.shape)],
        out_specs=[row] * 6 + [col] * 2,
        out_shape=[rows_shape] * 6 + [cols_shape] * 2,
        compiler_params=_params(2),
        name="inproj_prompt",
    )(x, norm_in, w_all, w_kv_t)


def _t5_bucket_np(dist):
    nf = np.maximum(dist, 1).astype(np.float32)
    large = MAX_EXACT + (np.log(nf / np.float32(MAX_EXACT)) / np.float32(math.log(MAX_DISTANCE / MAX_EXACT))
                         * np.float32(N_BUCKETS - MAX_EXACT)).astype(np.int32)
    large = np.minimum(large, N_BUCKETS - 1)
    return np.where(dist < MAX_EXACT, dist, large).astype(np.int32)


def _bucket_index_tables(n_new, wb):
    i = np.arange(BLOCK)[:, None]
    j = np.arange(2 * BLOCK)[None, :]
    diff = i - j + BLOCK
    band = (diff >= 0) & (diff <= N_STEPS)
    t = np.arange(n_new)[:, None]
    dist_old = wb + t - np.arange(wb)[None, :]
    dist_new = t - np.arange(LANES)[None, :]
    prompt, s_old, s_new = [], [], []
    for dil in DILATIONS:
        prompt.append(np.where(band, _t5_bucket_np(np.maximum(diff, 0) * dil), -1))
        for dist, out, extra in ((dist_old, s_old, True), (dist_new, s_new, np.arange(LANES)[None, :] < n_new)):
            ok = (dist >= 0) & (dist % dil == 0) & (dist // dil <= N_STEPS) & extra
            out.append(np.where(ok, _t5_bucket_np(np.maximum(dist, 0)), -1))
    as_i32 = lambda a: np.stack(a).astype(np.int32)
    return as_i32(prompt), as_i32(s_old), as_i32(s_new)


def _bias_tables_body(rb_ref, *refs, buckets):
    idx_refs, out_refs = refs[:3], refs[3:]
    for d in range(N_DIL):
        idx = [r[d] for r in idx_refs]

        def per_head(h, carry):
            tabs = [jnp.full(ix.shape, NEG_INF, F32) for ix in idx]
            for b in buckets[d]:
                val = rb_ref[b, h]
                tabs = [jnp.where(ix == b, val, tb) for ix, tb in zip(idx, tabs)]
            for o_ref, tb in zip(out_refs, tabs):
                o_ref[d, h] = tb
            return carry
        lax.fori_loop(0, N_HEADS, per_head, 0)


def _bias_tables(rel_bias, n_new, wb):
    idx = _bucket_index_tables(n_new, wb)
    buckets = tuple(tuple(int(b) for b in np.unique(np.concatenate([a[d].ravel() for a in idx])) if b >= 0)
                    for d in range(N_DIL))
    vmem = pl.BlockSpec(memory_space=pltpu.VMEM)
    return pl.pallas_call(
        functools.partial(_bias_tables_body, buckets=buckets),
        in_specs=[pl.BlockSpec(memory_space=pltpu.SMEM)] + [vmem] * 3,
        out_specs=[vmem] * 3,
        out_shape=[jax.ShapeDtypeStruct((N_DIL, N_HEADS) + a.shape[1:], F32) for a in idx],
        compiler_params=pltpu.CompilerParams(vmem_limit_bytes=VMEM_LIMIT),
        name="bias_tables",
    )(rel_bias, *[jnp.asarray(a) for a in idx])


INTERLEAVE = 4


def _prompt_scratch(seq):
    n_blocks = seq // BLOCK
    return ([pltpu.VMEM((seq, LANES), F32)] * 9
            + [pltpu.VMEM((n_blocks, HEADS_PER_PAIR * BLOCK, 2 * BLOCK), BF16)])


def _attn_prompt_body(q_ref, k_ref, v_ref, bm_ref, o_ref, qg_ref, kg_ref, vg_ref,
                      acc16_ref, m16_ref, l16_ref, accp_ref, mp_ref, lp_ref, e_ref, *, seq, extra_work=None):
    group = seq // INTERLEAVE
    nb4 = group // BLOCK
    n_blocks = seq // BLOCK
    assert DILATIONS == (1, INTERLEAVE, INTERLEAVE * INTERLEAVE) and seq == DILATIONS[2] * BLOCK

    first = lax.broadcasted_iota(jnp.int32, (BLOCK, LANES), 1) < HEAD_DIM
    pick = lambda a: jnp.where(first, a[:BLOCK], a[BLOCK:])
    grouped = (qg_ref, kg_ref, vg_ref)
    natural = (q_ref, k_ref, v_ref)

    one_trip = jnp.minimum(pl.program_id(0) + 1, 1)

    def region(fn):
        def body(_, carry):
            fn()
            return carry
        lax.fori_loop(0, one_trip, body, 0)

    def rows_of(start, n, stride):
        return pl.ds(start, n) if stride == 1 else pl.ds(start, n, stride=stride)

    def scores(slot, dil_idx, refs, stride, q_start, k_start, with_prev):
        qs_ref, ks_ref, _ = refs
        q = qs_ref[rows_of(q_start, BLOCK, stride), :]
        q2 = jnp.concatenate([jnp.where(first, q, 0.0), jnp.where(first, 0.0, q)], axis=0).astype(BF16)
        nk = 2 * BLOCK if with_prev else BLOCK
        kb = ks_ref[rows_of(k_start, nk, stride), :].astype(BF16)
        bias = bm_ref[dil_idx] if with_prev else bm_ref[dil_idx, :, BLOCK:]
        s = lax.dot_general(q2, kb, NT_DIMS, preferred_element_type=F32) + bias
        m = jnp.max(s, axis=-1, keepdims=True)
        e_ref[slot, :, 0:nk] = jnp.exp(s - m).astype(BF16)
        return pick(m)

    def values(slot, refs, stride, k_start, with_prev):
        nk = 2 * BLOCK if with_prev else BLOCK
        vb = refs[2][rows_of(k_start, nk, stride), :].astype(BF16)
        v1 = jnp.concatenate([vb, jnp.ones((nk, LANES), BF16)], axis=1)
        acc = jnp.dot(e_ref[slot, :, 0:nk], v1, preferred_element_type=F32)
        return pick(acc[:, :LANES]), pick(acc[:, LANES:])

    def merge(parts):
        mx = functools.reduce(jnp.maximum, [m for _, m, _ in parts])
        scale = [jnp.exp(m - mx) for _, m, _ in parts]
        acc = functools.reduce(lambda x, y: x + y, [w * a for w, (a, _, _) in zip(scale, parts)])
        den = functools.reduce(lambda x, y: x + y, [w * l for w, (_, _, l) in zip(scale, parts)])
        return acc, mx, den

    d16_start = [(idx // INTERLEAVE) * group + idx % INTERLEAVE for idx in range(n_blocks)]
    d16_rows = [pl.ds(st, BLOCK, stride=INTERLEAVE) for st in d16_start]
    d4_blocks = [(g * group + n * BLOCK, n > 0) for g in range(INTERLEAVE) for n in range(nb4)]
    d4_rows = [pl.ds(g + n * BLOCK * INTERLEAVE, BLOCK, stride=INTERLEAVE) for g in range(INTERLEAVE) for n in range(nb4)]

    def setup_and_d16_scores():
        for src, dst in zip(natural, grouped):
            for g in range(INTERLEAVE):
                dst[g * group:(g + 1) * group, :] = src[pl.ds(g, group, stride=INTERLEAVE), :]
        for slot, st in enumerate(d16_start):
            m16_ref[d16_rows[slot], :] = scores(slot, 2, grouped, INTERLEAVE, st, st, False)

    def d16_values_and_d4_scores():
        for slot, st in enumerate(d16_start):
            acc, l = values(slot, grouped, INTERLEAVE, st, False)
            acc16_ref[d16_rows[slot], :] = acc
            l16_ref[d16_rows[slot], :] = l
        for slot, (st, with_prev) in enumerate(d4_blocks):
            accp_ref[d4_rows[slot], :] = scores(slot, 1, grouped, 1, st, st - BLOCK if with_prev else st, with_prev)

    def d4_values_and_d1_scores():
        for slot, (st, with_prev) in enumerate(d4_blocks):
            acc, l = values(slot, grouped, 1, st - BLOCK if with_prev else st, with_prev)
            rows = pl.ds(st, BLOCK)
            vals = merge([(acc, accp_ref[d4_rows[slot], :], l),
                          (acc16_ref[rows, :], m16_ref[rows, :], l16_ref[rows, :])])
            for ref, val in zip((accp_ref, mp_ref, lp_ref), vals):
                ref[d4_rows[slot], :] = val
        for n in range(n_blocks):
            o_ref[pl.ds(n * BLOCK, BLOCK), :] = scores(n, 0, natural, 1, n * BLOCK, max(n - 1, 0) * BLOCK, n > 0)

    def d1_values():
        for n in range(n_blocks):
            rows = pl.ds(n * BLOCK, BLOCK)
            acc, l = values(n, natural, 1, max(n - 1, 0) * BLOCK, n > 0)
            acc, _, den = merge([(acc, o_ref[rows, :], l), (accp_ref[rows, :], mp_ref[rows, :], lp_ref[rows, :])])
            o_ref[rows, :] = acc / den

    stages = (setup_and_d16_scores, d16_values_and_d4_scores, d4_values_and_d1_scores, d1_values)
    for stage, extra in zip(stages, extra_work or (None,) * len(stages)):
        region(stage if extra is None else lambda stage=stage, extra=extra: (extra(), stage()))


def _attn_prompt(q, k, v, bias_tabs):
    b, seq, _ = q.shape
    rows = pl.BlockSpec((None, seq, LANES), lambda p, i: (i, 0, p))
    scratch = _prompt_scratch(seq)
    return pl.pallas_call(
        functools.partial(_attn_prompt_body, seq=seq),
        grid=(N_PAIRS, b),
        in_specs=[rows, rows, rows,
                  pl.BlockSpec((N_DIL, None, HEADS_PER_PAIR * BLOCK, 2 * BLOCK), lambda p, i: (0, p, 0, 0))],
        out_specs=rows,
        out_shape=jax.ShapeDtypeStruct((b, seq, W_A), F32),
        scratch_shapes=scratch,
        compiler_params=_params(2),
        name="attn_prompt",
    )(q, k, v, bias_tabs)


SAMPLE_HEADS = 8
SAMPLE_LANES = SAMPLE_HEADS * HEAD_DIM


def _padded_new(new_ref, n_new):
    return jnp.concatenate([new_ref[...], jnp.zeros((LANES - n_new, SAMPLE_LANES), F32)], axis=0)


def _window_update(c_ref, new_ref, out_ref, *, wb, n_new, rows=(0, SAMPLE_LANES)):
    lo, hi = rows
    last = lax.broadcasted_iota(jnp.int32, (hi - lo, LANES), 1) >= LANES - n_new
    rolled = pltpu.roll(c_ref[lo:hi, :], wb - n_new, axis=1)
    out_ref[lo:hi, :] = rolled
    new_t = _padded_new(new_ref, n_new).T[lo:hi, :]
    tail = pltpu.roll(new_t, LANES - n_new, axis=1)
    out_ref[lo:hi, wb - LANES:wb] = jnp.where(last, tail, rolled[:, wb - LANES:wb])


def _own_head(n_new):
    rows = SAMPLE_HEADS * n_new
    row_head = lax.broadcasted_iota(jnp.int32, (rows, SAMPLE_LANES), 0) // n_new
    col_head = lax.broadcasted_iota(jnp.int32, (rows, SAMPLE_LANES), 1) // HEAD_DIM
    return row_head == col_head


def _sample_scores(q_ref, kn_ref, ckt_ref, bo_ref, bn_ref, eo_ref, en_ref, den_ref, *, n_new, group):
    q_rep = jnp.concatenate([q_ref[...]] * SAMPLE_HEADS, axis=0)
    q_bd = jnp.where(_own_head(n_new), q_rep, 0.0).astype(BF16)
    kn = _padded_new(kn_ref, n_new)
    s_old = jnp.dot(q_bd, ckt_ref[...].astype(BF16), preferred_element_type=F32)
    s_new = lax.dot_general(q_bd, kn.astype(BF16), NT_DIMS, preferred_element_type=F32)
    rows = SAMPLE_HEADS * n_new
    own = pl.ds(pl.multiple_of(group * rows, rows), rows)
    lo = [s_old + bo_ref[d, own, :] for d in range(N_DIL)]
    ln = [s_new + bn_ref[d, own, :] for d in range(N_DIL)]
    row_max = lambda t: jnp.max(t, axis=-1, keepdims=True)
    mx = functools.reduce(jnp.maximum, [row_max(t) for t in lo + ln])
    e_old = functools.reduce(lambda a, b: a + b, [jnp.exp(t - mx) for t in lo])
    e_new = functools.reduce(lambda a, b: a + b, [jnp.exp(t - mx) for t in ln])
    den = jnp.sum(e_old, axis=-1, keepdims=True) + jnp.sum(e_new, axis=-1, keepdims=True)
    eo_ref[...] = e_old.astype(BF16)
    en_ref[...] = e_new.astype(BF16)
    den_ref[...] = jnp.broadcast_to(den, den_ref.shape)


def _sample_values(vn_ref, cvt_ref, eo_ref, en_ref, den_ref, o_ref, *, n_new):
    vn = _padded_new(vn_ref, n_new)
    o_full = lax.dot_general(eo_ref[...], cvt_ref[...].astype(BF16), NT_DIMS, preferred_element_type=F32)
    o_full = o_full + jnp.dot(en_ref[...], vn.astype(BF16), preferred_element_type=F32)
    o_full = jnp.where(_own_head(n_new), o_full / den_ref[:, 0:1], 0.0)
    out = o_full[0:n_new, :]
    for h in range(1, SAMPLE_HEADS):
        out = out + o_full[h * n_new:(h + 1) * n_new, :]
    o_ref[...] = out


def _sample_scratch(wb, n_new):
    rows = SAMPLE_HEADS * n_new
    return [pltpu.VMEM((rows, wb), BF16), pltpu.VMEM((rows, LANES), BF16), pltpu.VMEM((rows, LANES), F32)]


N_PROMPT_IN, N_SAMPLE_IN, N_SAMPLE_OUT = 4, 7, 3
VMEM_LIMIT_BOTH = 62 * 1024 * 1024


def _attn_both_body(*refs, seq, wb, n_new):
    n_in = N_PROMPT_IN + N_SAMPLE_IN
    prompt_in, sample_in = refs[:N_PROMPT_IN], refs[N_PROMPT_IN:n_in]
    o_ref, sample_out, scratch = refs[n_in], refs[n_in + 1:n_in + 1 + N_SAMPLE_OUT], refs[n_in + 1 + N_SAMPLE_OUT:]
    q_ref, kn_ref, vn_ref, ckt_ref, cvt_ref, bo_ref, bn_ref = sample_in
    os_ref, nkt_ref, nvt_ref = sample_out
    n_sample_scratch = len(_sample_scratch(wb, n_new))
    prompt_scratch, sample_scratch = scratch[:-n_sample_scratch], scratch[-n_sample_scratch:]
    half = SAMPLE_LANES // 2
    update_k = functools.partial(_window_update, ckt_ref, kn_ref, nkt_ref, wb=wb, n_new=n_new)
    update_v = functools.partial(_window_update, cvt_ref, vn_ref, nvt_ref, wb=wb, n_new=n_new)
    group = (pl.program_id(0) * pl.num_programs(1) + pl.program_id(1)) % (N_HEADS // SAMPLE_HEADS)
    score = functools.partial(_sample_scores, q_ref, kn_ref, ckt_ref, bo_ref, bn_ref, *sample_scratch,
                              n_new=n_new, group=group)
    value = functools.partial(_sample_values, vn_ref, cvt_ref, *sample_scratch, os_ref, n_new=n_new)
    extra_work = (None, functools.partial(update_k, rows=(0, half)), functools.partial(update_v, rows=(0, half)),
                  lambda: (update_k(rows=(half, SAMPLE_LANES)), update_v(rows=(half, SAMPLE_LANES)), score(), value()))
    _attn_prompt_body(*prompt_in, o_ref, *prompt_scratch, seq=seq, extra_work=extra_work)


def _attn_both(q, k, v, bias_prompt, qs, k_new, v_new, cache_kt, cache_vt, bias_old, bias_new):
    bp, seq, _ = q.shape
    bs, n_new, _ = qs.shape
    wb = cache_kt.shape[2]
    n_groups = N_HEADS // SAMPLE_HEADS
    assert N_PAIRS * bp == bs * n_groups, "one sample (batch, head group) per prompt (head pair, batch) step"
    step = lambda p, i: p * bp + i
    rows = pl.BlockSpec((None, seq, LANES), lambda p, i: (i, 0, p))
    new_spec = pl.BlockSpec((None, n_new, SAMPLE_LANES), lambda p, i: (step(p, i) // n_groups, 0, step(p, i) % n_groups))
    cache_spec = pl.BlockSpec((None, SAMPLE_LANES, wb), lambda p, i: (step(p, i) // n_groups, step(p, i) % n_groups, 0))
    scratch = _prompt_scratch(seq) + _sample_scratch(wb, n_new)
    return pl.pallas_call(
        functools.partial(_attn_both_body, seq=seq, wb=wb, n_new=n_new),
        grid=(N_PAIRS, bp),
        in_specs=[rows, rows, rows,
                  pl.BlockSpec((N_DIL, None, HEADS_PER_PAIR * BLOCK, 2 * BLOCK), lambda p, i: (0, p, 0, 0)),
                  new_spec, new_spec, new_spec, cache_spec, cache_spec,
                  _resident(bias_old.shape), _resident(bias_new.shape)],
        out_specs=[rows, new_spec, cache_spec, cache_spec],
        out_shape=[jax.ShapeDtypeStruct((bp, seq, W_A), F32),
                   jax.ShapeDtypeStruct((bs, n_new, W_A), F32),
                   jax.ShapeDtypeStruct((bs, W_A, wb), F32),
                   jax.ShapeDtypeStruct((bs, W_A, wb), F32)],
        scratch_shapes=scratch,
        compiler_params=_params(2, VMEM_LIMIT_BOTH),
        name="attn_both",
    )(q, k, v, bias_prompt, qs, k_new, v_new, cache_kt, cache_vt, bias_old, bias_new)


def _lru_core(xr_ref, gr_ref, cpast_ref, hpast_ref, cw_ref, cb_ref, wg_ref, bgx_ref, bga_ref, lp_ref, nl_ref,
              nconv_ref, nlru_ref, tail_ref, h_ref, hs_ref, tt):
    t = pl.program_id(1)
    n_t = pl.num_programs(1)

    @pl.when(t == 0)
    def _():
        tail_ref[...] = jnp.concatenate(
            [jnp.zeros((SUBLANES - (CONV_W - 1), W_R), F32), cpast_ref[...]], axis=0)
        h_ref[...] = hpast_ref[...]

    x = xr_ref[...]
    xe = jnp.concatenate([tail_ref[...], x], axis=0)
    xc = cb_ref[...] + cw_ref[CONV_W - 1:CONV_W, :] * x
    for tap in range(CONV_W - 1):
        back = CONV_W - 1 - tap
        xc = xc + cw_ref[tap:tap + 1, :] * xe[SUBLANES - back:SUBLANES - back + tt, :]
    tail_ref[...] = x[tt - SUBLANES:tt, :]

    xcb = xc.astype(BF16)
    gx, ga = [], []
    for j in range(W_R // MXU_TILE):
        g = jnp.dot(xcb[:, j * MXU_TILE:(j + 1) * MXU_TILE], wg_ref[j], preferred_element_type=F32)
        gx.append(g[:, :MXU_TILE])
        ga.append(g[:, MXU_TILE:])
    gate_x = _sigmoid(jnp.concatenate(gx, axis=-1) + bgx_ref[...])
    gate_a = _sigmoid(jnp.concatenate(ga, axis=-1) + bga_ref[...])
    log_a = gate_a * ((-LRU_C) * jax.nn.softplus(-lp_ref[...]))
    a = jnp.exp(log_a)
    var = jnp.tanh(-log_a) * (1.0 + a * a)
    root = jnp.where(var > 0.0, var * lax.rsqrt(var), 0.0)
    bx = root * (gate_x * xc)

    groups = tt // SUBLANES
    a3 = a.reshape(groups, SUBLANES, W_R)
    b3 = bx.reshape(groups, SUBLANES, W_R)
    sub = lax.broadcasted_iota(jnp.int32, (groups, SUBLANES, W_R), 1)
    shift = 1
    while shift < SUBLANES:
        keep = sub >= shift
        b_prev = jnp.where(keep, pltpu.roll(b3, shift, axis=1), 0.0)
        a_prev = jnp.where(keep, pltpu.roll(a3, shift, axis=1), 1.0)
        b3 = b3 + a3 * b_prev
        a3 = a3 * a_prev
        shift *= 2
    h = h_ref[...]
    for g in range(groups):
        hg = a3[g] * h + b3[g]
        hs_ref[g * SUBLANES:(g + 1) * SUBLANES, :] = hg
        h = hg[SUBLANES - 1:SUBLANES, :]
    h_ref[...] = h

    @pl.when(t == n_t - 1)
    def _():
        nconv_ref[...] = x[tt - (CONV_W - 1):tt, :]
        nlru_ref[...] = h

    o = hs_ref[...]
    return (((o * _rms_scale(o)) * nl_ref[...]) * _silu(gr_ref[...])).astype(BF16)


N_LRU_IN = 11


N_LRU_PER_BATCH = 4


def _lru_body(*refs, tt, nb):
    ins, (y_ref, nconv_ref, nlru_ref), scratch = refs[:N_LRU_IN], refs[N_LRU_IN:N_LRU_IN + 3], refs[N_LRU_IN + 3:]
    if nb is None:
        y_ref[...] = _lru_core(*ins, nconv_ref, nlru_ref, *scratch, tt)
        return
    for bi in range(nb):
        ins_b = [r.at[bi] if k < N_LRU_PER_BATCH else r for k, r in enumerate(ins)]
        y_ref[bi] = _lru_core(*ins_b, nconv_ref.at[bi], nlru_ref.at[bi], *scratch, tt)


def _lru_specs(b, t_len, tt, nb=None):
    assert t_len % tt == 0 and tt % SUBLANES == 0 and tt >= SUBLANES
    assert nb is None or (t_len == tt and b % nb == 0)
    seq_spec = pl.BlockSpec((nb, tt, W_R), lambda i, j: (i, j, 0))
    vec = pl.BlockSpec((1, W_R), lambda i, j: (0, 0))
    conv_spec = pl.BlockSpec((nb, CONV_W - 1, W_R), lambda i, j: (i, 0, 0))
    h_spec = pl.BlockSpec((nb, 1, W_R), lambda i, j: (i, 0, 0))
    in_specs = [seq_spec, seq_spec, conv_spec, h_spec,
                pl.BlockSpec((CONV_W, W_R), lambda i, j: (0, 0)), vec,
                pl.BlockSpec((W_R // MXU_TILE, MXU_TILE, 2 * MXU_TILE), lambda i, j: (0, 0, 0)),
                vec, vec, vec, vec]
    state_shapes = [jax.ShapeDtypeStruct((b, CONV_W - 1, W_R), F32), jax.ShapeDtypeStruct((b, 1, W_R), F32)]
    scratch = [pltpu.VMEM((SUBLANES, W_R), F32), pltpu.VMEM((1, W_R), F32), pltpu.VMEM((tt, W_R), F32)]
    return seq_spec, vec, in_specs, [conv_spec, h_spec], state_shapes, scratch


def _lru(x_r, g_r, conv_past, h_past, conv_w, conv_b, w_gates, b_gate_x, b_gate_a, lru_param, norm_lru, tt, nb=None):
    b, t_len, _ = x_r.shape
    seq_spec, _, in_specs, state_specs, state_shapes, scratch = _lru_specs(b, t_len, tt, nb)
    return pl.pallas_call(
        functools.partial(_lru_body, tt=tt, nb=nb),
        grid=(b // (nb or 1), t_len // tt),
        in_specs=in_specs,
        out_specs=[seq_spec] + state_specs,
        out_shape=[jax.ShapeDtypeStruct((b, t_len, W_R), BF16)] + state_shapes,
        scratch_shapes=scratch,
        compiler_params=_params(2),
        name="rglru",
    )(x_r, g_r, conv_past, h_past, conv_w, conv_b, w_gates, b_gate_x, b_gate_a, lru_param, norm_lru)


def _gate_weight_tiles(w_gate_x, w_gate_a):
    per_tile = MXU_TILE // LRU_BLOCK
    eye = jnp.eye(per_tile, dtype=F32)

    def tiles(w):
        w = w.reshape(N_LRU_BLOCKS // per_tile, per_tile, LRU_BLOCK, LRU_BLOCK)
        bd = jnp.einsum('jaik,ab->jaibk', w, eye)
        return bd.reshape(N_LRU_BLOCKS // per_tile, MXU_TILE, MXU_TILE)
    return jnp.concatenate([tiles(w_gate_x), tiles(w_gate_a)], axis=-1).astype(BF16)


def _outproj_core(o_ref, ga_ref, y_lru, x_ref, na_ref, nf_ref, w_ref):
    o = o_ref[...]
    ya = ((o * _rms_scale(o)) * na_ref[...]) * _silu(ga_ref[...])
    acc = jnp.dot(ya.astype(BF16), w_ref[0:W_A, :], preferred_element_type=F32)
    acc = acc + jnp.dot(y_lru, w_ref[W_A:W_A + W_R, :], preferred_element_type=F32)
    y = x_ref[...] + acc
    return (y * _rms_scale(y)) * nf_ref[...]


def _outproj_body(o_ref, ga_ref, yl_ref, x_ref, na_ref, nf_ref, w_ref, y_ref):
    y_ref[...] = _outproj_core(o_ref, ga_ref, yl_ref[...], x_ref, na_ref, nf_ref, w_ref)


N_OUT_IN = 6


def _lru_outproj_body(*refs, tt):
    lru_in, out_in = refs[:N_LRU_IN], refs[N_LRU_IN:N_LRU_IN + N_OUT_IN]
    y_ref, nconv_ref, nlru_ref = refs[N_LRU_IN + N_OUT_IN:N_LRU_IN + N_OUT_IN + 3]
    scratch = refs[N_LRU_IN + N_OUT_IN + 3:]
    o_ref, ga_ref, x_ref, na_ref, nf_ref, w_ref = out_in
    y_lru = _lru_core(*lru_in, nconv_ref, nlru_ref, *scratch, tt)
    y_ref[...] = _outproj_core(o_ref, ga_ref, y_lru, x_ref, na_ref, nf_ref, w_ref)


def _lru_outproj(x_r, g_r, conv_past, h_past, lru_params, o_att, g_a, x, norm_attn, norm_final, w_out_bf16, tt):
    b, t_len, _ = x_r.shape
    seq_spec, vec, in_specs, state_specs, state_shapes, scratch = _lru_specs(b, t_len, tt)
    return pl.pallas_call(
        functools.partial(_lru_outproj_body, tt=tt),
        grid=(b, t_len // tt),
        in_specs=in_specs + [seq_spec, seq_spec, seq_spec, vec, vec, _resident(w_out_bf16.shape)],
        out_specs=[seq_spec] + state_specs,
        out_shape=[jax.ShapeDtypeStruct((b, t_len, D_MODEL), F32)] + state_shapes,
        scratch_shapes=scratch,
        compiler_params=_params(2),
        name="rglru_outproj",
    )(x_r, g_r, conv_past, h_past, *lru_params, o_att, g_a, x, norm_attn, norm_final, w_out_bf16)


def _outproj(o_att, g_a, y_lru, x2d, norm_attn, norm_final, w_out_bf16, tm):
    m = x2d.shape[0]
    row = pl.BlockSpec((tm, D_MODEL), lambda i: (i, 0))
    vec = pl.BlockSpec((1, D_MODEL), lambda i: (0, 0))
    return pl.pallas_call(
        _outproj_body,
        grid=(m // tm,),
        in_specs=[row, row, row, row, vec, vec, pl.BlockSpec((W_A + W_R, D_MODEL), lambda i: (0, 0))],
        out_specs=row,
        out_shape=jax.ShapeDtypeStruct((m, D_MODEL), F32),
        compiler_params=_params(1),
        name="outproj",
    )(o_att, g_a, y_lru, x2d, norm_attn, norm_final, w_out_bf16)


PROMPT_ROWS = 512
LRU_TILE = 512
SAMPLE_LRU_BATCH = 8


def _to_head_major(win):
    b, p = win.shape[0], win.shape[1]
    return jnp.transpose(win, (0, 2, 3, 1)).reshape(b, W_A, p)


def _from_head_major(win_t):
    b, _, p = win_t.shape
    return jnp.transpose(win_t.reshape(b, N_HEADS, HEAD_DIM, p), (0, 3, 1, 2))[None]


def kernel(x_prompt, x_sample, cache_win_k, cache_win_v, state_conv, state_lru, rel_bias, norm_in, w_in, norm_attn, norm_lru, conv_w, conv_b, w_gate_x, b_gate_x, w_gate_a, b_gate_a, lru_param, w_out, norm_final):
    depth = w_in.shape[0]
    assert depth == 1, "single-layer trunk"
    bp, seq, _ = x_prompt.shape
    bs, n_new, _ = x_sample.shape
    wb = cache_win_k.shape[2]
    row = lambda p: p.reshape(1, -1)

    w_in_b = w_in[0].astype(BF16)
    w_kv_t = w_in_b[:, W_A:3 * W_A].T
    w_out_b = w_out[0].astype(BF16)
    w_gates = _gate_weight_tiles(w_gate_x[0], w_gate_a[0])
    lru_args = (conv_w[0], row(conv_b[0]), w_gates, row(b_gate_x[0]), row(b_gate_a[0]), row(lru_param[0]), row(norm_lru[0]))
    bias_prompt, bias_old, bias_new = _bias_tables(rel_bias, n_new, wb)

    q, g_a, x_r, g_r, k, v, k_t, v_t = _inproj_prompt(x_prompt, row(norm_in[0]), w_in_b, w_kv_t, PROMPT_ROWS)
    xs = x_sample.reshape(bs * n_new, D_MODEL)
    qs, ks, vs, gas, xrs, grs = _inproj_rows(xs, row(norm_in[0]), w_in_b)
    as_new = lambda a: a.reshape(bs, n_new, -1)

    o_att, o_s, new_kt, new_vt = _attn_both(
        q, k, v, bias_prompt.reshape(N_DIL, N_PAIRS, HEADS_PER_PAIR * BLOCK, 2 * BLOCK),
        as_new(qs), as_new(ks), as_new(vs), _to_head_major(cache_win_k[0]), _to_head_major(cache_win_v[0]),
        bias_old.reshape(N_DIL, N_HEADS * n_new, wb), bias_new.reshape(N_DIL, N_HEADS * n_new, LANES))

    y_prompt, conv_p, lru_p = _lru_outproj(
        x_r, g_r, jnp.zeros((bp, CONV_W - 1, W_R), F32), jnp.zeros((bp, 1, W_R), state_lru.dtype), lru_args,
        o_att, g_a, x_prompt, row(norm_attn[0]), row(norm_final), w_out_b, tt=LRU_TILE)
    y_lru_s, conv_s, lru_s = _lru(as_new(xrs), as_new(grs), state_conv[0], state_lru[0].reshape(bs, 1, W_R),
                                  *lru_args, tt=n_new, nb=SAMPLE_LRU_BATCH)
    y_sample = _outproj(o_s.reshape(bs * n_new, W_A), gas, y_lru_s.reshape(bs * n_new, W_R), xs,
                        row(norm_attn[0]), row(norm_final), w_out_b, bs * n_new)

    return (y_prompt.reshape(bp, seq, D_MODEL), y_sample.reshape(bs, n_new, D_MODEL),
            _from_head_major(k_t), _from_head_major(v_t), conv_p[None], lru_p.reshape(1, bp, W_R),
            _from_head_major(new_kt), _from_head_major(new_vt), conv_s[None],
            lru_s.reshape(1, bs, W_R).astype(state_lru.dtype))
```
